```python
import math
import jax
import jax.numpy as jnp
from jax import lax
import numpy as np

D_MODEL = 1024
BATCH = 8
SEQ = 2048
DEPTH = 4
DEC_BATCH = 8
DEC_SEQ = 16
PAST_LEN = 1024

CHUNK = 64
Q_BLOCK = 128
EPS = 1e-6
FOX_HEADS = 8
FOX_HEAD_DIM = 128
FOX_WIDTH = FOX_HEADS * FOX_HEAD_DIM
FOX_FORGET_BIAS = 3.0
LRU_WIDTH = 1024
LRU_BLOCKS = 8
LRU_BLOCK_DIM = LRU_WIDTH // LRU_BLOCKS
CONV_WIDTH = 4
LRU_C = 8.0
GLA_HEADS = 4
GLA_DK = 128
GLA_DV = 256
GLA_QK_WIDTH = GLA_HEADS * GLA_DK
GLA_V_WIDTH = GLA_HEADS * GLA_DV
GLA_RANK = 16
GLA_TAU = 16.0
GLA_CHUNK = CHUNK
N_BRANCH = 3
BRANCH_WIDTH = 1024
D_FF = 4 * D_MODEL

SPLIT_SIZES = (FOX_WIDTH, FOX_WIDTH, FOX_WIDTH, FOX_HEADS,
               LRU_WIDTH, LRU_WIDTH,
               GLA_QK_WIDTH, GLA_QK_WIDTH, GLA_V_WIDTH, GLA_RANK, GLA_V_WIDTH,
               N_BRANCH * D_MODEL)
D_IN = FOX_WIDTH * 3 + FOX_HEADS + LRU_WIDTH * 2 + GLA_QK_WIDTH * 2 + GLA_V_WIDTH * 2 + GLA_RANK + N_BRANCH * D_MODEL

kernel_name = "hybrid_fox_rglru_gla_stream_step"


def _split_points():
    pts, acc = [], 0
    for s in SPLIT_SIZES[:-1]:
        acc += s
        pts.append(acc)
    return pts


def rms_norm(x, g):
    xf = x.astype(jnp.float32)
    y = xf * lax.rsqrt(jnp.mean(xf * xf, axis=-1, keepdims=True) + EPS)
    return (y * g.astype(jnp.float32)).astype(x.dtype)


def fox_attend(q, k, v, cq, ck, q_pos, k_pos):
    s = jnp.einsum('bqhd,bkhd->bhqk', q, k).astype(jnp.float32) * (FOX_HEAD_DIM ** -0.5)
    s = s + jnp.swapaxes(cq, 1, 2)[..., :, None] - jnp.swapaxes(ck, 1, 2)[..., None, :]
    mask = k_pos[None, :] <= q_pos[:, None]
    s = jnp.where(mask, s, -jnp.inf)
    p = jax.nn.softmax(s, axis=-1)
    return jnp.einsum('bhqk,bkhd->bqhd', p.astype(v.dtype), v)


def fox_prompt(q, k, v, c):
    B, T, H, Dh = q.shape
    nb = T // Q_BLOCK
    qb = q.reshape(B, nb, Q_BLOCK, H, Dh).swapaxes(0, 1)
    cb = c.reshape(B, nb, Q_BLOCK, H).swapaxes(0, 1)
    k_pos = jnp.arange(T)

    def block(args):
        i, qi, ci = args
        q_pos = i * Q_BLOCK + jnp.arange(Q_BLOCK)
        return fox_attend(qi, k, v, ci, c, q_pos, k_pos)

    o = lax.map(block, (jnp.arange(nb), qb, cb))
    return o.swapaxes(0, 1).reshape(B, T, H, Dh)


def causal_conv(x, buf, w, b):
    T = x.shape[1]
    xp = jnp.concatenate([buf, x], axis=1)
    y = b + sum(xp[:, j:j + T] * w[j] for j in range(CONV_WIDTH))
    return y, xp[:, -(CONV_WIDTH - 1):]


def rg_lru(x, h0, w_a, b_a, w_x, b_x, lam):
    B, T, W = x.shape
    xb = x.reshape(B, T, LRU_BLOCKS, LRU_BLOCK_DIM)
    r = jax.nn.sigmoid((jnp.einsum('btni,nij->btnj', xb, w_a).reshape(B, T, W) + b_a).astype(jnp.float32))
    i = jax.nn.sigmoid((jnp.einsum('btni,nij->btnj', xb, w_x).reshape(B, T, W) + b_x).astype(jnp.float32))
    log_a = LRU_C * r * jax.nn.log_sigmoid(lam.astype(jnp.float32))
    a = jnp.exp(log_a)
    mult = jnp.sqrt(-jnp.expm1(2.0 * log_a))
    u = mult * i * x.astype(jnp.float32)
    u = u.at[:, 0].add(a[:, 0] * h0)

    def combine(left, right):
        a1, b1 = left
        a2, b2 = right
        return a1 * a2, a2 * b1 + b2

    _, h = lax.associative_scan(combine, (a, u), axis=1)
    return h, h[:, -1]


def gla_chunked(q, k, v, log_a, S0):
    B, T, H, DK = q.shape
    DV = v.shape[-1]
    C = min(GLA_CHUNK, T)
    n = T // C
    f32 = jnp.float32
    qc = (q.astype(f32) * (DK ** -0.5)).reshape(B, n, C, H, DK)
    kc = k.astype(f32).reshape(B, n, C, H, DK)
    vc = v.astype(f32).reshape(B, n, C, H, DV)
    bcum = jnp.cumsum(log_a.reshape(B, n, C, H, DK), axis=2)
    b_last = bcum[:, :, -1:]
    q_t = qc * jnp.exp(bcum)
    k_t = kc * jnp.exp(-bcum)
    k_end = kc * jnp.exp(b_last - bcum)
    causal = jnp.tril(jnp.ones((C, C), dtype=bool))
    A = jnp.einsum('bnthd,bnshd->bnhts', q_t, k_t)
    A = jnp.where(causal, A, 0.0)
    o_intra = jnp.einsum('bnhts,bnshv->bnthv', A, vc)
    dS = jnp.einsum('bnshd,bnshv->bnhdv', k_end, vc)
    decay = jnp.exp(b_last[:, :, 0])

    def step(S, inp):
        dS_i, dec_i = inp
        return dec_i[..., None] * S + dS_i, S

    S_final, S_starts = lax.scan(step, S0, (jnp.moveaxis(dS, 1, 0), jnp.moveaxis(decay, 1, 0)))
    S_starts = jnp.moveaxis(S_starts, 0, 1)
    o_inter = jnp.einsum('bnthd,bnhdv->bnthv', q_t, S_starts)
    return (o_intra + o_inter).reshape(B, T, H, DV), S_final


def _layer(x, p, past):
    B, T, _ = x.shape
    f32 = jnp.float32
    xn = rms_norm(x, p['norm_mix_pre'])
    z = xn @ p['w_in']
    fq, fk, fv, ff, lx, lg, gq, gk, gv, glr, gog, mg = jnp.split(z, _split_points(), axis=-1)

    q = fq.reshape(B, T, FOX_HEADS, FOX_HEAD_DIM)
    k = fk.reshape(B, T, FOX_HEADS, FOX_HEAD_DIM)
    v = fv.reshape(B, T, FOX_HEADS, FOX_HEAD_DIM)
    logf = jax.nn.log_sigmoid((ff + p['b_fox_f']).astype(f32))
    if past is None:
        c = jnp.cumsum(logf, axis=1)
        o = fox_prompt(q, k, v, c)
        conv_buf = jnp.zeros((B, CONV_WIDTH - 1, LRU_WIDTH), x.dtype)
        h0 = jnp.zeros((B, LRU_WIDTH), f32)
        S0 = jnp.zeros((B, GLA_HEADS, GLA_DK, GLA_DV), f32)
    else:
        k_c, v_c, lf_c, conv_buf, h0, S0 = past
        P = k_c.shape[1]
        k_all = jnp.concatenate([k_c.astype(k.dtype), k], axis=1)
        v_all = jnp.concatenate([v_c.astype(v.dtype), v], axis=1)
        c_all = jnp.cumsum(jnp.concatenate([lf_c.astype(f32), logf], axis=1), axis=1)
        o = fox_attend(q, k_all, v_all, c_all[:, P:], c_all, P + jnp.arange(T), jnp.arange(P + T))
        conv_buf = conv_buf.astype(x.dtype)
        h0 = h0.astype(f32)
        S0 = S0.astype(f32)
    y_fox = o.reshape(B, T, FOX_WIDTH)

    xc, conv_new = causal_conv(lx, conv_buf, p['lru_conv_w'], p['lru_conv_b'])
    hseq, h_last = rg_lru(xc, h0, p['lru_w_a'], p['lru_b_a'], p['lru_w_x'], p['lru_b_x'], p['lru_lambda'])
    y_lru = jax.nn.gelu(lg) * hseq.astype(x.dtype)

    gqh = gq.reshape(B, T, GLA_HEADS, GLA_DK)
    gkh = gk.reshape(B, T, GLA_HEADS, GLA_DK)
    gvh = gv.reshape(B, T, GLA_HEADS, GLA_DV)
    gate_logit = (glr @ p['gla_w_gate_up'] + p['gla_b_gate']).astype(f32)
    log_a = (jax.nn.log_sigmoid(gate_logit) / GLA_TAU).reshape(B, T, GLA_HEADS, GLA_DK)
    o_gla, S_new = gla_chunked(gqh, gkh, gvh, log_a, S0)
    o_gla = rms_norm(o_gla, p['gla_norm']).astype(x.dtype).reshape(B, T, GLA_V_WIDTH)
    y_gla = o_gla * jax.nn.silu(gog)

    branches = jnp.stack([y_fox, y_lru, y_gla], axis=2)
    proj = jnp.einsum('btnc,ncd->btnd', branches, p['w_branch'])
    gates = jax.nn.sigmoid(mg).reshape(B, T, N_BRANCH, D_MODEL)
    mix = jnp.sum(gates * proj, axis=2) @ p['w_out']
    x = x + rms_norm(mix, p['norm_mix_post'])

    hn = rms_norm(x, p['norm_mlp_pre'])
    u = jnp.square(jax.nn.relu(hn @ p['w_up'])) @ p['w_down']
    x = x + rms_norm(u, p['norm_mlp_post'])
    new_state = (k, v, logf.astype(x.dtype), conv_new, h_last.astype(x.dtype), S_new.astype(x.dtype))
    return x, new_state


def setup_inputs(seed: int = 0) -> dict:
    key = jax.random.key(seed)
    ks = iter(jax.random.split(key, 32))
    f32 = jnp.float32

    def nrm(shape, scale=1.0):
        return scale * jax.random.normal(next(ks), shape, f32)

    x_prompt = nrm((BATCH, SEQ, D_MODEL))
    x_sample = nrm((DEC_BATCH, DEC_SEQ, D_MODEL))
    cache_fox_k = nrm((DEPTH, DEC_BATCH, PAST_LEN, FOX_HEADS, FOX_HEAD_DIM))
    cache_fox_v = nrm((DEPTH, DEC_BATCH, PAST_LEN, FOX_HEADS, FOX_HEAD_DIM))
    cache_fox_logf = jax.nn.log_sigmoid(FOX_FORGET_BIAS + nrm((DEPTH, DEC_BATCH, PAST_LEN, FOX_HEADS)))
    state_lru_conv = nrm((DEPTH, DEC_BATCH, CONV_WIDTH - 1, LRU_WIDTH))
    state_lru_h = nrm((DEPTH, DEC_BATCH, LRU_WIDTH), 0.5)
    state_gla = nrm((DEPTH, DEC_BATCH, GLA_HEADS, GLA_DK, GLA_DV), 2.0)
    norm_mix_pre = 1.0 + nrm((DEPTH, D_MODEL), 0.05)
    norm_mix_post = 1.0 + nrm((DEPTH, D_MODEL), 0.05)
    w_in = nrm((DEPTH, D_MODEL, D_IN), D_MODEL ** -0.5)
    b_fox_f = FOX_FORGET_BIAS + nrm((DEPTH, FOX_HEADS), 0.1)
    lru_conv_w = nrm((DEPTH, CONV_WIDTH, LRU_WIDTH), CONV_WIDTH ** -0.5)
    lru_conv_b = nrm((DEPTH, LRU_WIDTH), 0.02)
    lru_w_a = nrm((DEPTH, LRU_BLOCKS, LRU_BLOCK_DIM, LRU_BLOCK_DIM), LRU_BLOCK_DIM ** -0.5)
    lru_b_a = nrm((DEPTH, LRU_WIDTH), 0.02)
    lru_w_x = nrm((DEPTH, LRU_BLOCKS, LRU_BLOCK_DIM, LRU_BLOCK_DIM), LRU_BLOCK_DIM ** -0.5)
    lru_b_x = nrm((DEPTH, LRU_WIDTH), 0.02)
    u = jax.random.uniform(next(ks), (DEPTH, LRU_WIDTH), f32, 0.9, 0.999)
    s = u ** (1.0 / LRU_C)
    lru_lambda = jnp.log(s) - jnp.log1p(-s)
    gla_w_gate_up = nrm((DEPTH, GLA_RANK, GLA_QK_WIDTH), GLA_RANK ** -0.5)
    gla_b_gate = nrm((DEPTH, GLA_QK_WIDTH), 0.02)
    gla_norm = 1.0 + nrm((DEPTH, GLA_DV), 0.05)
    w_branch = nrm((DEPTH, N_BRANCH, BRANCH_WIDTH, D_MODEL), BRANCH_WIDTH ** -0.5)
    w_out = nrm((DEPTH, D_MODEL, D_MODEL), D_MODEL ** -0.5)
    norm_mlp_pre = 1.0 + nrm((DEPTH, D_MODEL), 0.05)
    norm_mlp_post = 1.0 + nrm((DEPTH, D_MODEL), 0.05)
    w_up = nrm((DEPTH, D_MODEL, D_FF), D_MODEL ** -0.5)
    w_down = nrm((DEPTH, D_FF, D_MODEL), D_FF ** -0.5)
    return {
        'x_prompt': x_prompt, 'x_sample': x_sample,
        'cache_fox_k': cache_fox_k, 'cache_fox_v': cache_fox_v, 'cache_fox_logf': cache_fox_logf,
        'state_lru_conv': state_lru_conv, 'state_lru_h': state_lru_h, 'state_gla': state_gla,
        'norm_mix_pre': norm_mix_pre, 'norm_mix_post': norm_mix_post, 'w_in': w_in, 'b_fox_f': b_fox_f,
        'lru_conv_w': lru_conv_w, 'lru_conv_b': lru_conv_b, 'lru_w_a': lru_w_a, 'lru_b_a': lru_b_a,
        'lru_w_x': lru_w_x, 'lru_b_x': lru_b_x, 'lru_lambda': lru_lambda,
        'gla_w_gate_up': gla_w_gate_up, 'gla_b_gate': gla_b_gate, 'gla_norm': gla_norm,
        'w_branch': w_branch, 'w_out': w_out, 'norm_mlp_pre': norm_mlp_pre, 'norm_mlp_post': norm_mlp_post,
        'w_up': w_up, 'w_down': w_down,
    }


def reference(x_prompt, x_sample, cache_fox_k, cache_fox_v, cache_fox_logf, state_lru_conv, state_lru_h,
              state_gla, norm_mix_pre, norm_mix_post, w_in, b_fox_f, lru_conv_w, lru_conv_b, lru_w_a, lru_b_a,
              lru_w_x, lru_b_x, lru_lambda, gla_w_gate_up, gla_b_gate, gla_norm, w_branch, w_out,
              norm_mlp_pre, norm_mlp_post, w_up, w_down):
    hp, hs = x_prompt, x_sample
    new_p = [[] for _ in range(6)]
    new_s = [[] for _ in range(6)]
    for l in range(DEPTH):
        p = {
            'norm_mix_pre': norm_mix_pre[l], 'norm_mix_post': norm_mix_post[l], 'w_in': w_in[l],
            'b_fox_f': b_fox_f[l], 'lru_conv_w': lru_conv_w[l], 'lru_conv_b': lru_conv_b[l],
            'lru_w_a': lru_w_a[l], 'lru_b_a': lru_b_a[l], 'lru_w_x': lru_w_x[l], 'lru_b_x': lru_b_x[l],
            'lru_lambda': lru_lambda[l], 'gla_w_gate_up': gla_w_gate_up[l], 'gla_b_gate': gla_b_gate[l],
            'gla_norm': gla_norm[l], 'w_branch': w_branch[l], 'w_out': w_out[l],
            'norm_mlp_pre': norm_mlp_pre[l], 'norm_mlp_post': norm_mlp_post[l],
            'w_up': w_up[l], 'w_down': w_down[l],
        }
        hp, st_p = _layer(hp, p, None)
        past = (cache_fox_k[l], cache_fox_v[l], cache_fox_logf[l], state_lru_conv[l], state_lru_h[l], state_gla[l])
        hs, st_s = _layer(hs, p, past)
        for lst, a in zip(new_p, st_p):
            lst.append(a)
        for lst, a in zip(new_s, st_s):
            lst.append(a)
    fox_k_p, fox_v_p, fox_lf_p, lru_conv_p, lru_h_p, gla_s_p = [jnp.stack(a) for a in new_p]
    fox_k_s, fox_v_s, fox_lf_s, lru_conv_s, lru_h_s, gla_s_s = [jnp.stack(a) for a in new_s]
    return (hp, hs, fox_k_p, fox_v_p, fox_lf_p, lru_conv_p, lru_h_p, gla_s_p,
            fox_k_s, fox_v_s, fox_lf_s, lru_conv_s, lru_h_s, gla_s_s)
```

```python
import functools
import math

import jax
import jax.numpy as jnp
from jax import lax
from jax.experimental import pallas as pl
from jax.experimental.pallas import tpu as pltpu

F32 = jnp.float32
BF16 = jnp.bfloat16

EPS = 1e-6
CHUNK = 64
FOX_HEADS = 8
FOX_HEAD_DIM = 128
FOX_WIDTH = FOX_HEADS * FOX_HEAD_DIM
LRU_WIDTH = 1024
LRU_BLOCKS = 8
LRU_BLOCK_DIM = LRU_WIDTH // LRU_BLOCKS
CONV_WIDTH = 4
LRU_C = 8.0
GLA_HEADS = 4
GLA_DK = 128
GLA_DV = 256
GLA_QK_WIDTH = GLA_HEADS * GLA_DK
GLA_V_WIDTH = GLA_HEADS * GLA_DV
GLA_RANK = 16
GLA_TAU = 16.0
N_BRANCH = 3
LANES = 128
SUBLANES = 8
VMEM_LIMIT = 56 * 1024 * 1024

MISC_WIDTH = LANES
MISC_FF = 0
MISC_GLR = FOX_HEADS


def _log_sigmoid(x):
    return jnp.minimum(x, 0.0) - jnp.log1p(jnp.exp(-jnp.abs(x)))


def _rms(x, g):
    return x * lax.rsqrt(jnp.mean(x * x, axis=-1, keepdims=True) + EPS) * g


def _cumsum(x, axis):
    n = x.shape[axis]
    idx = lax.broadcasted_iota(jnp.int32, x.shape, axis)
    k = 1
    while k < n:
        x = x + jnp.where(idx >= k, pltpu.roll(x, k, axis), 0.0)
        k *= 2
    return x


def _params(n_axes):
    return pltpu.CompilerParams(dimension_semantics=("arbitrary",) * n_axes,
                                vmem_limit_bytes=VMEM_LIMIT)


def _resident(shape):
    nd = len(shape)
    return pl.BlockSpec(shape, lambda *_: (0,) * nd, pipeline_mode=pl.Buffered(1))


def _proj_kernel(x_ref, g_ref, w_ref, *out_refs, groups, col_tile):
    xb = _rms(x_ref[...], g_ref[...]).astype(BF16)
    for (c0, width, scale, outs) in groups:
        for s in range(0, width, col_tile):
            w = min(col_tile, width - s)
            z = jnp.dot(xb, w_ref[:, c0 + s:c0 + s + w], preferred_element_type=F32)
            if scale != 1.0:
                z = z * scale
            for oi in outs:
                out_refs[oi][:, s:s + w] = z.astype(out_refs[oi].dtype)


def _proj(x, g, w, tm):
    n, d = x.shape
    widths = [FOX_WIDTH, FOX_WIDTH, FOX_WIDTH, LRU_WIDTH, LRU_WIDTH, GLA_QK_WIDTH, GLA_QK_WIDTH,
              GLA_V_WIDTH, GLA_V_WIDTH, N_BRANCH * d, MISC_WIDTH]
    offs = [sum(widths[:i]) for i in range(len(widths))]
    out_defs = [(FOX_WIDTH, BF16),
                (FOX_WIDTH, F32),
                (FOX_WIDTH, F32),
                (FOX_WIDTH, BF16),
                (FOX_WIDTH, BF16),
                (LRU_WIDTH, BF16),
                (LRU_WIDTH, BF16),
                (GLA_QK_WIDTH, BF16),
                (GLA_QK_WIDTH, BF16),
                (GLA_V_WIDTH, BF16),
                (GLA_V_WIDTH, BF16),
                (N_BRANCH * d, BF16),
                (MISC_WIDTH, F32)]
    groups = ((offs[0], widths[0], FOX_HEAD_DIM ** -0.5, (0,)),
              (offs[1], widths[1], 1.0, (1, 3)),
              (offs[2], widths[2], 1.0, (2, 4)),
              (offs[3], widths[3], 1.0, (5,)),
              (offs[4], widths[4], 1.0, (6,)),
              (offs[5], widths[5], 1.0, (7,)),
              (offs[6], widths[6], 1.0, (8,)),
              (offs[7], widths[7], 1.0, (9,)),
              (offs[8], widths[8], 1.0, (10,)),
              (offs[9], widths[9], 1.0, (11,)),
              (offs[10], widths[10], 1.0, (12,)))
    return pl.pallas_call(
        functools.partial(_proj_kernel, groups=groups, col_tile=1024),
        grid=(n // tm,),
        in_specs=[pl.BlockSpec((tm, d), lambda i: (i, 0)),
                  _resident((1, d)),
                  _resident(w.shape)],
        out_specs=[pl.BlockSpec((tm, wd), lambda i: (i, 0)) for wd, _ in out_defs],
        out_shape=[jax.ShapeDtypeStruct((n, wd), dt) for wd, dt in out_defs],
        compiler_params=_params(1),
        name="proj",
    )(x, g, w)


def _pack_w_in(w_in):
    sizes = (FOX_WIDTH, FOX_WIDTH, FOX_WIDTH, FOX_HEADS, LRU_WIDTH, LRU_WIDTH, GLA_QK_WIDTH, GLA_QK_WIDTH,
             GLA_V_WIDTH, GLA_RANK, GLA_V_WIDTH, N_BRANCH * w_in.shape[1])
    pts, acc = [], 0
    for s in sizes:
        pts.append((acc, acc + s))
        acc += s
    col = lambda i: w_in[:, :, pts[i][0]:pts[i][1]]
    pad = jnp.zeros(w_in.shape[:2] + (MISC_WIDTH - FOX_HEADS - GLA_RANK,), w_in.dtype)
    parts = [col(0), col(1), col(2), col(4), col(5), col(6), col(7), col(8), col(10), col(11),
             col(3), col(9), pad]
    return jnp.concatenate(parts, axis=-1).astype(BF16)


def _gate_kernel(misc_ref, fft_ref, b_ref, bt_ref, c0_ref, c0t_ref, logf_ref, c_ref, ct_ref, *, apply_ls):
    z = misc_ref[...] + b_ref[...]
    zt = fft_ref[...] + bt_ref[...]
    if apply_ls:
        z = _log_sigmoid(z)
        zt = _log_sigmoid(zt)
    logf_ref[...] = z[:, :FOX_HEADS]
    c_ref[...] = (_cumsum(z, 0) + c0_ref[...])[:, :FOX_HEADS]
    ct_ref[...] = _cumsum(zt, 1) + c0t_ref[...]


def _gate(misc, fft, b, bt, c0, c0t, apply_ls):
    bsz, t, _ = misc.shape
    h = FOX_HEADS
    tl = fft.shape[2]
    return pl.pallas_call(
        functools.partial(_gate_kernel, apply_ls=apply_ls),
        grid=(bsz,),
        in_specs=[pl.BlockSpec((None, t, MISC_WIDTH), lambda i: (i, 0, 0)),
                  pl.BlockSpec((None, h, tl), lambda i: (i, 0, 0)),
                  pl.BlockSpec((1, MISC_WIDTH), lambda i: (0, 0)),
                  pl.BlockSpec((h, 1), lambda i: (0, 0)),
                  pl.BlockSpec((None, 1, MISC_WIDTH), lambda i: (i, 0, 0)),
                  pl.BlockSpec((None, h, 1), lambda i: (i, 0, 0))],
        out_specs=[pl.BlockSpec((None, t, h), lambda i: (i, 0, 0)),
                   pl.BlockSpec((None, t, h), lambda i: (i, 0, 0)),
                   pl.BlockSpec((None, h, tl), lambda i: (i, 0, 0))],
        out_shape=[jax.ShapeDtypeStruct((bsz, t, h), F32),
                   jax.ShapeDtypeStruct((bsz, t, h), F32),
                   jax.ShapeDtypeStruct((bsz, h, tl), F32)],
        compiler_params=_params(1),
        name="gate",
    )(misc, fft, b, bt, c0, c0t)


def _fox_kernel(qi_tab, ki_tab, q_ref, k_ref, v_ref, cq_ref, ckt_ref, o_ref, m_scr, l_scr, acc_scr, *, tq):
    p = pl.program_id(1)
    qi = qi_tab[p]
    ki = ki_tab[p]
    dh = FOX_HEAD_DIM

    @pl.when(ki == 0)
    def _init():
        m_scr[...] = jnp.full(m_scr.shape, -jnp.inf, F32)
        l_scr[...] = jnp.zeros(l_scr.shape, F32)
        acc_scr[...] = jnp.zeros(acc_scr.shape, F32)

    def step(masked):
        if masked:
            row = lax.broadcasted_iota(jnp.int32, (tq, tq), 0)
            col = lax.broadcasted_iota(jnp.int32, (tq, tq), 1)
            keep = col <= row
        for h in range(FOX_HEADS):
            sl = slice(h * dh, (h + 1) * dh)
            s = lax.dot_general(q_ref[:, sl], k_ref[:, sl], (((1,), (1,)), ((), ())),
                                preferred_element_type=F32)
            s = s + (cq_ref[:, h:h + 1] - ckt_ref[h:h + 1, :])
            if masked:
                s = jnp.where(keep, s, -jnp.inf)
            m_old = m_scr[h]
            m_new = jnp.maximum(m_old, jnp.max(s, axis=1, keepdims=True))
            alpha = jnp.exp(m_old - m_new)
            pr = jnp.exp(s - m_new)
            l_scr[h] = alpha * l_scr[h] + jnp.sum(pr, axis=1, keepdims=True)
            pv = jnp.dot(pr.astype(BF16), v_ref[:, sl], preferred_element_type=F32)
            acc_scr[:, sl] = alpha * acc_scr[:, sl] + pv
            m_scr[h] = m_new

    @pl.when(ki < qi)
    def _off():
        step(False)

    @pl.when(ki == qi)
    def _diag():
        step(True)
        for h in range(FOX_HEADS):
            sl = slice(h * dh, (h + 1) * dh)
            o_ref[:, sl] = (acc_scr[:, sl] / l_scr[h]).astype(o_ref.dtype)


def _fox_prompt(q, kb, vb, c, ct, bsz, t, tq):
    nq = t // tq
    pairs = [(i, j) for i in range(nq) for j in range(i + 1)]
    qi_tab = jnp.asarray([pq for pq, _ in pairs], jnp.int32)
    ki_tab = jnp.asarray([pk for _, pk in pairs], jnp.int32)
    w = FOX_WIDTH
    grid_spec = pltpu.PrefetchScalarGridSpec(
        num_scalar_prefetch=2,
        grid=(bsz, len(pairs)),
        in_specs=[pl.BlockSpec((tq, w), lambda b, p, qt, kt: (b * nq + qt[p], 0)),
                  pl.BlockSpec((tq, w), lambda b, p, qt, kt: (b * nq + kt[p], 0)),
                  pl.BlockSpec((tq, w), lambda b, p, qt, kt: (b * nq + kt[p], 0)),
                  pl.BlockSpec((tq, FOX_HEADS), lambda b, p, qt, kt: (b * nq + qt[p], 0)),
                  pl.BlockSpec((None, FOX_HEADS, tq), lambda b, p, qt, kt: (b, 0, kt[p]))],
        out_specs=pl.BlockSpec((tq, w), lambda b, p, qt, kt: (b * nq + qt[p], 0)),
        scratch_shapes=[pltpu.VMEM((FOX_HEADS, tq, 1), F32),
                        pltpu.VMEM((FOX_HEADS, tq, 1), F32),
                        pltpu.VMEM((tq, w), F32)])
    return pl.pallas_call(
        functools.partial(_fox_kernel, tq=tq),
        grid_spec=grid_spec,
        out_shape=jax.ShapeDtypeStruct((bsz * t, w), BF16),
        compiler_params=_params(2),
        name="fox_prompt",
    )(qi_tab, ki_tab, q, kb, vb, c, ct)


def _fox_dec_kernel(q_ref, kn_ref, vn_ref, kc_ref, vc_ref, cq_ref, cnt_ref, cpt_ref, o_ref):
    dh = FOX_HEAD_DIM
    t = q_ref.shape[0]
    row = lax.broadcasted_iota(jnp.int32, (t, t), 0)
    col = lax.broadcasted_iota(jnp.int32, (t, t), 1)
    keep = col <= row
    nt = (((1,), (1,)), ((), ()))
    for h in range(FOX_HEADS):
        sl = slice(h * dh, (h + 1) * dh)
        q = q_ref[:, sl]
        cq = cq_ref[:, h:h + 1]
        sp = lax.dot_general(q, kc_ref[:, sl].astype(BF16), nt, preferred_element_type=F32)
        sp = sp + (cq - cpt_ref[h:h + 1, :])
        sn = lax.dot_general(q, kn_ref[:, sl], nt, preferred_element_type=F32)
        sn = sn + (cq - cnt_ref[h:h + 1, :t])
        sn = jnp.where(keep, sn, -jnp.inf)
        m = jnp.maximum(jnp.max(sp, axis=1, keepdims=True), jnp.max(sn, axis=1, keepdims=True))
        pp = jnp.exp(sp - m)
        pn = jnp.exp(sn - m)
        l = jnp.sum(pp, axis=1, keepdims=True) + jnp.sum(pn, axis=1, keepdims=True)
        o = (jnp.dot(pp.astype(BF16), vc_ref[:, sl].astype(BF16), preferred_element_type=F32)
             + jnp.dot(pn.astype(BF16), vn_ref[:, sl], preferred_element_type=F32))
        o_ref[:, sl] = (o / l).astype(o_ref.dtype)


def _fox_decode(q, kn, vn, kc, vc, cq, cnt, cpt):
    bsz, t, w = q.shape
    p = kc.shape[1]
    h = FOX_HEADS
    blk3 = lambda s1, s2: pl.BlockSpec((None, s1, s2), lambda b: (b, 0, 0))
    return pl.pallas_call(
        _fox_dec_kernel,
        grid=(bsz,),
        in_specs=[blk3(t, w), blk3(t, w), blk3(t, w), blk3(p, w), blk3(p, w),
                  blk3(t, h), blk3(h, cnt.shape[2]), blk3(h, p)],
        out_specs=blk3(t, w),
        out_shape=jax.ShapeDtypeStruct((bsz, t, w), BF16),
        compiler_params=_params(1),
        name="fox_decode",
    )(q, kn, vn, kc, vc, cq, cnt, cpt)


def _lru_kernel(lx_ref, lg_ref, buf_ref, h0_ref, cw_ref, cb_ref, wa_ref, ba_ref, wx_ref, bx_ref, lam_ref,
                y_ref, hl_ref, halo_scr, h_scr, a_scr, u_scr, *, tc, pitch):
    step = pl.program_id(0)
    bsz = lx_ref.shape[0]
    nblk = LRU_BLOCKS
    bd = LRU_BLOCK_DIM

    @pl.when(step == 0)
    def _init():
        halo_scr[...] = jnp.zeros(halo_scr.shape, F32)
        halo_scr[:, SUBLANES - (CONV_WIDTH - 1):, :] = buf_ref[...]
        h_scr[...] = h0_ref[...]

    log_lam = LRU_C * _log_sigmoid(lam_ref[...])
    for b in range(bsz):
        x = lx_ref[b].astype(F32)
        xp = jnp.concatenate([halo_scr[b], x], axis=0)
        xc = cb_ref[...] + cw_ref[CONV_WIDTH - 1:CONV_WIDTH, :] * x
        for j in range(1, CONV_WIDTH):
            shifted = pltpu.roll(xp, j, 0)[SUBLANES:, :]
            xc = xc + cw_ref[CONV_WIDTH - 1 - j:CONV_WIDTH - j, :] * shifted
        halo_scr[b] = x[tc - SUBLANES:, :] if tc >= SUBLANES else xp[tc:, :]
        for n in range(nblk):
            sl = slice(n * bd, (n + 1) * bd)
            xcn = xc[:, sl]
            xb = xcn.astype(BF16)
            r = jax.nn.sigmoid(jnp.dot(xb, wa_ref[n], preferred_element_type=F32) + ba_ref[:, sl])
            ig = jax.nn.sigmoid(jnp.dot(xb, wx_ref[n], preferred_element_type=F32) + bx_ref[:, sl])
            log_a = r * log_lam[:, sl]
            a = jnp.exp(log_a)
            mult = jnp.sqrt(-jnp.tanh(log_a) * (a * a + 1.0))
            a_scr[n, b * pitch:b * pitch + tc, :] = a
            u_scr[n, b * pitch:b * pitch + tc, :] = mult * ig * xcn

    def scan_body(t, hs):
        new = []
        for n in range(nblk):
            idx = pl.ds(t, bsz, stride=pitch)
            hn = a_scr[n, idx, :] * hs[n] + u_scr[n, idx, :]
            u_scr[n, idx, :] = hn
            new.append(hn)
        return tuple(new)

    h_init = tuple(h_scr[:, n * bd:(n + 1) * bd] for n in range(nblk))
    h_fin = lax.fori_loop(0, tc, scan_body, h_init)
    for n in range(nblk):
        h_scr[:, n * bd:(n + 1) * bd] = h_fin[n]
    hl_ref[...] = h_scr[...]

    for b in range(bsz):
        for n in range(nblk):
            sl = slice(n * bd, (n + 1) * bd)
            hseq = u_scr[n, b * pitch:b * pitch + tc, :]
            gate = jax.nn.gelu(lg_ref[b, :, sl].astype(F32), approximate=True)
            y_ref[b, :, sl] = (gate * hseq).astype(y_ref.dtype)


def _lru(lx, lg, buf, h0, cw, cb, wa, ba, wx, bx, lam, tc):
    bsz, t, w = lx.shape
    pitch = tc + 4
    rows = bsz * pitch
    rows = (rows + SUBLANES - 1) // SUBLANES * SUBLANES
    full = lambda shape: pl.BlockSpec(shape, lambda i: (0,) * len(shape))
    return pl.pallas_call(
        functools.partial(_lru_kernel, tc=tc, pitch=pitch),
        grid=(t // tc,),
        in_specs=[pl.BlockSpec((bsz, tc, w), lambda i: (0, i, 0)),
                  pl.BlockSpec((bsz, tc, w), lambda i: (0, i, 0)),
                  full(buf.shape), full(h0.shape), full(cw.shape), full(cb.shape),
                  full(wa.shape), full(ba.shape), full(wx.shape), full(bx.shape), full(lam.shape)],
        out_specs=[pl.BlockSpec((bsz, tc, w), lambda i: (0, i, 0)),
                   full((bsz, w))],
        out_shape=[jax.ShapeDtypeStruct((bsz, t, w), BF16),
                   jax.ShapeDtypeStruct((bsz, w), F32)],
        scratch_shapes=[pltpu.VMEM((bsz, SUBLANES, w), F32),
                        pltpu.VMEM((bsz, w), F32),
                        pltpu.VMEM((LRU_BLOCKS, rows, LRU_BLOCK_DIM), F32),
                        pltpu.VMEM((LRU_BLOCKS, rows, LRU_BLOCK_DIM), F32)],
        compiler_params=_params(1),
        name="lru",
    )(lx, lg, buf, h0, cw, cb, wa, ba, wx, bx, lam)


def _gla_kernel(gq_ref, gk_ref, gv_ref, gog_ref, misc_ref, wg_ref, bg_ref, gn_ref, s0_ref,
                y_ref, st_ref, *, chunk, n_sub):
    @pl.when(pl.program_id(1) == 0)
    def _init():
        st_ref[...] = s0_ref[...]

    dk, dv = GLA_DK, GLA_DV
    row = lax.broadcasted_iota(jnp.int32, (chunk, chunk), 0)
    col = lax.broadcasted_iota(jnp.int32, (chunk, chunk), 1)
    causal = col <= row
    nt = (((1,), (1,)), ((), ()))
    tn = (((0,), (0,)), ((), ()))
    for c in range(n_sub):
        rs = slice(c * chunk, (c + 1) * chunk)
        logit = jnp.dot(misc_ref[rs, :].astype(BF16), wg_ref[...], preferred_element_type=F32) + bg_ref[...]
        log_a = _log_sigmoid(logit) / GLA_TAU
        bcum = _cumsum(log_a, 0)
        b_last = bcum[chunk - 1:chunk, :]
        e_q = jnp.exp(bcum) * (dk ** -0.5)
        e_k = jnp.exp(-bcum)
        e_end = jnp.exp(b_last - bcum)
        decay = jnp.exp(b_last)
        qf = gq_ref[rs, :].astype(F32)
        kf = gk_ref[rs, :].astype(F32)
        for h in range(GLA_HEADS):
            ks = slice(h * dk, (h + 1) * dk)
            vs = slice(h * dv, (h + 1) * dv)
            q_t = (qf[:, ks] * e_q[:, ks]).astype(BF16)
            k_t = (kf[:, ks] * e_k[:, ks]).astype(BF16)
            k_e = (kf[:, ks] * e_end[:, ks]).astype(BF16)
            v = gv_ref[rs, vs]
            st = st_ref[h]
            a = lax.dot_general(q_t, k_t, nt, preferred_element_type=F32)
            a = jnp.where(causal, a, 0.0)
            o = (jnp.dot(a.astype(BF16), v, preferred_element_type=F32)
                 + lax.dot_general(q_t, st.astype(BF16), nt, preferred_element_type=F32))
            st_ref[h] = decay[:, ks] * st + lax.dot_general(v, k_e, tn, preferred_element_type=F32)
            on = _rms(o, gn_ref[...])
            g = gog_ref[rs, vs].astype(F32)
            y_ref[rs, vs] = (on * (g * jax.nn.sigmoid(g))).astype(y_ref.dtype)


def _gla(gq, gk, gv, gog, misc, wg, bg, gn, s0t, bsz, t, tcg, chunk):
    nc = t // tcg
    row_blk = lambda wd: pl.BlockSpec((tcg, wd), lambda b, i: (b * nc + i, 0))
    full = lambda shape: pl.BlockSpec(shape, lambda b, i: (0,) * len(shape))
    st_blk = pl.BlockSpec((None, GLA_HEADS, GLA_DV, GLA_DK), lambda b, i: (b, 0, 0, 0))
    return pl.pallas_call(
        functools.partial(_gla_kernel, chunk=chunk, n_sub=tcg // chunk),
        grid=(bsz, nc),
        in_specs=[row_blk(GLA_QK_WIDTH), row_blk(GLA_QK_WIDTH), row_blk(GLA_V_WIDTH), row_blk(GLA_V_WIDTH),
                  row_blk(MISC_WIDTH), full(wg.shape), full(bg.shape), full(gn.shape), st_blk],
        out_specs=[row_blk(GLA_V_WIDTH), st_blk],
        out_shape=[jax.ShapeDtypeStruct((bsz * t, GLA_V_WIDTH), BF16),
                   jax.ShapeDtypeStruct((bsz, GLA_HEADS, GLA_DV, GLA_DK), F32)],
        compiler_params=_params(2),
        name="gla",
    )(gq, gk, gv, gog, misc, wg, bg, gn, s0t)


def _mix_kernel(x_ref, yf_ref, yl_ref, yg_ref, mg_ref, wb_ref, wo_ref, wu_ref, wd_ref,
                g1_ref, g2_ref, g3_ref, o_ref):
    d = x_ref.shape[1]
    mix = None
    for i, y_ref in enumerate((yf_ref, yl_ref, yg_ref)):
        gate = jax.nn.sigmoid(mg_ref[:, i * d:(i + 1) * d].astype(F32))
        term = gate * jnp.dot(y_ref[...], wb_ref[i], preferred_element_type=F32)
        mix = term if mix is None else mix + term
    mo = jnp.dot(mix.astype(BF16), wo_ref[...], preferred_element_type=F32)
    x1 = x_ref[...] + _rms(mo, g1_ref[...])
    hn = _rms(x1, g2_ref[...]).astype(BF16)
    up = jnp.dot(hn, wu_ref[...], preferred_element_type=F32)
    act = jnp.square(jnp.maximum(up, 0.0)).astype(BF16)
    dn = jnp.dot(act, wd_ref[...], preferred_element_type=F32)
    o_ref[...] = x1 + _rms(dn, g3_ref[...])


def _mix(x, yf, yl, yg, mg, wb, wo, wu, wd, g1, g2, g3, tm):
    n, d = x.shape
    row = lambda wd_: pl.BlockSpec((tm, wd_), lambda i: (i, 0))
    return pl.pallas_call(
        _mix_kernel,
        grid=(n // tm,),
        in_specs=[row(d), row(d), row(d), row(d), row(N_BRANCH * d),
                  _resident(wb.shape), _resident(wo.shape), _resident(wu.shape), _resident(wd.shape),
                  _resident(g1.shape), _resident(g2.shape), _resident(g3.shape)],
        out_specs=row(d),
        out_shape=jax.ShapeDtypeStruct((n, d), F32),
        compiler_params=_params(1),
        name="mix",
    )(x, yf, yl, yg, mg, wb, wo, wu, wd, g1, g2, g3)


def _layer(x, p, past, bsz, t):
    d = x.shape[1]
    prompt = past is None
    tm = min(256, bsz * t)
    q, k, v, kb, vb, lx, lg, gq, gk, gv, gog, mg, misc = _proj(x, p['g_pre'], p['w_in'], tm)

    misc3 = misc.reshape(bsz, t, MISC_WIDTH)
    fft = jnp.swapaxes(misc3[:, :, MISC_FF:MISC_FF + FOX_HEADS], 1, 2)
    if prompt:
        zc = jnp.zeros((bsz, 1, MISC_WIDTH), F32)
        zct = jnp.zeros((bsz, FOX_HEADS, 1), F32)
        logf, c, ct = _gate(misc3, fft, p['b_ff'], p['b_ff_t'], zc, zct, True)
        y_fox = _fox_prompt(q, kb, vb, c.reshape(bsz * t, FOX_HEADS), ct, bsz, t, min(256, t))
        conv_buf = jnp.zeros((bsz, CONV_WIDTH - 1, LRU_WIDTH), F32)
        h0 = jnp.zeros((bsz, LRU_WIDTH), F32)
        s0t = jnp.zeros((bsz, GLA_HEADS, GLA_DV, GLA_DK), F32)
    else:
        k_c, v_c, lf_c, conv_buf, h0, s0 = past
        pl_len = k_c.shape[1]
        k_c = k_c.reshape(bsz, pl_len, FOX_WIDTH)
        v_c = v_c.reshape(bsz, pl_len, FOX_WIDTH)
        lf_pad = jnp.pad(lf_c, ((0, 0), (0, 0), (0, MISC_WIDTH - FOX_HEADS)))
        zb = jnp.zeros((1, MISC_WIDTH), F32)
        zbt = jnp.zeros((FOX_HEADS, 1), F32)
        zc = jnp.zeros((bsz, 1, MISC_WIDTH), F32)
        zct = jnp.zeros((bsz, FOX_HEADS, 1), F32)
        _, c_past, cpt = _gate(lf_pad, jnp.swapaxes(lf_c, 1, 2), zb, zbt, zc, zct, False)
        tot = jnp.pad(c_past[:, pl_len - 1:pl_len, :], ((0, 0), (0, 0), (0, MISC_WIDTH - FOX_HEADS)))
        tot_t = cpt[:, :, pl_len - 1:pl_len]
        fft_pad = jnp.pad(fft, ((0, 0), (0, 0), (0, LANES - t)))
        logf, c, cnt = _gate(misc3, fft_pad, p['b_ff'], p['b_ff_t'], tot, tot_t, True)
        y_fox = _fox_decode(q.reshape(bsz, t, FOX_WIDTH), kb.reshape(bsz, t, FOX_WIDTH),
                            vb.reshape(bsz, t, FOX_WIDTH), k_c, v_c, c, cnt, cpt)
        y_fox = y_fox.reshape(bsz * t, FOX_WIDTH)
        s0t = jnp.swapaxes(s0, 2, 3)

    lx3 = lx.reshape(bsz, t, LRU_WIDTH)
    y_lru, h_last = _lru(lx3, lg.reshape(bsz, t, LRU_WIDTH), conv_buf, h0, p['conv_w'], p['conv_b'],
                         p['w_a'], p['b_a'], p['w_x'], p['b_x'], p['lam'], min(128, t))
    assert t >= CONV_WIDTH - 1
    conv_new = lx3[:, t - (CONV_WIDTH - 1):].astype(F32)

    chunk = min(CHUNK, t)
    y_gla, s_t = _gla(gq, gk, gv, gog, misc, p['w_gate'], p['b_gate'], p['gla_norm'], s0t,
                      bsz, t, min(256, t), chunk)
    s_new = jnp.swapaxes(s_t, 2, 3)

    x = _mix(x, y_fox, y_lru.reshape(bsz * t, LRU_WIDTH), y_gla, mg, p['w_branch'], p['w_out'],
             p['w_up'], p['w_down'], p['g_post'], p['g_mlp_pre'], p['g_mlp_post'], tm)
    new_state = (k.reshape(bsz, t, FOX_HEADS, FOX_HEAD_DIM), v.reshape(bsz, t, FOX_HEADS, FOX_HEAD_DIM),
                 logf, conv_new, h_last, s_new)
    return x, new_state


def kernel(x_prompt, x_sample, cache_fox_k, cache_fox_v, cache_fox_logf, state_lru_conv, state_lru_h, state_gla, norm_mix_pre, norm_mix_post, w_in, b_fox_f, lru_conv_w, lru_conv_b, lru_w_a, lru_b_a, lru_w_x, lru_b_x, lru_lambda, gla_w_gate_up, gla_b_gate, gla_norm, w_branch, w_out, norm_mlp_pre, norm_mlp_post, w_up, w_down):
    depth, d = norm_mix_pre.shape
    bp, tp, _ = x_prompt.shape
    bs, ts, _ = x_sample.shape

    w_in_p = _pack_w_in(w_in)
    row = lambda a: a.reshape(depth, 1, a.shape[-1])
    b_ff = jnp.pad(b_fox_f, ((0, 0), (0, MISC_WIDTH - FOX_HEADS))).reshape(depth, 1, MISC_WIDTH)
    b_ff_t = b_fox_f.reshape(depth, FOX_HEADS, 1)
    w_gate = jnp.zeros((depth, MISC_WIDTH, GLA_QK_WIDTH), F32)
    w_gate = w_gate.at[:, MISC_GLR:MISC_GLR + GLA_RANK, :].set(gla_w_gate_up).astype(BF16)
    layers = []
    for l in range(depth):
        layers.append({
            'g_pre': row(norm_mix_pre)[l], 'w_in': w_in_p[l],
            'b_ff': b_ff[l], 'b_ff_t': b_ff_t[l],
            'conv_w': lru_conv_w[l], 'conv_b': row(lru_conv_b)[l],
            'w_a': lru_w_a[l].astype(BF16), 'b_a': row(lru_b_a)[l],
            'w_x': lru_w_x[l].astype(BF16), 'b_x': row(lru_b_x)[l], 'lam': row(lru_lambda)[l],
            'w_gate': w_gate[l], 'b_gate': row(gla_b_gate)[l], 'gla_norm': row(gla_norm)[l],
            'w_branch': w_branch[l].astype(BF16), 'w_out': w_out[l].astype(BF16),
            'w_up': w_up[l].astype(BF16), 'w_down': w_down[l].astype(BF16),
            'g_post': row(norm_mix_post)[l], 'g_mlp_pre': row(norm_mlp_pre)[l],
            'g_mlp_post': row(norm_mlp_post)[l],
        })

    hp = x_prompt.reshape(bp * tp, d)
    hs = x_sample.reshape(bs * ts, d)
    new_p = [[] for _ in range(6)]
    new_s = [[] for _ in range(6)]
    for l in range(depth):
        hp, st_p = _layer(hp, layers[l], None, bp, tp)
        past = (cache_fox_k[l], cache_fox_v[l], cache_fox_logf[l], state_lru_conv[l], state_lru_h[l],
                state_gla[l])
        hs, st_s = _layer(hs, layers[l], past, bs, ts)
        for lst, a in zip(new_p, st_p):
            lst.append(a)
        for lst, a in zip(new_s, st_s):
            lst.append(a)
    outs_p = [jnp.stack(a) for a in new_p]
    outs_s = [jnp.stack(a) for a in new_s]
    return (hp.reshape(bp, tp, d), hs.reshape(bs, ts, d), *outs_p, *outs_s)
```

```python
import functools
import math

import jax
import jax.numpy as jnp
from jax import lax
from jax.experimental import pallas as pl
from jax.experimental.pallas import tpu as pltpu

F32 = jnp.float32
BF16 = jnp.bfloat16

EPS = 1e-6
CHUNK = 64
FOX_HEADS = 8
FOX_HEAD_DIM = 128
FOX_WIDTH = FOX_HEADS * FOX_HEAD_DIM
LRU_WIDTH = 1024
LRU_BLOCKS = 8
LRU_BLOCK_DIM = LRU_WIDTH // LRU_BLOCKS
CONV_WIDTH = 4
LRU_C = 8.0
GLA_HEADS = 4
GLA_DK = 128
GLA_DV = 256
GLA_QK_WIDTH = GLA_HEADS * GLA_DK
GLA_V_WIDTH = GLA_HEADS * GLA_DV
GLA_RANK = 16
GLA_TAU = 16.0
N_BRANCH = 3
LANES = 128
SUBLANES = 8
VMEM_LIMIT = 56 * 1024 * 1024

MISC_WIDTH = LANES
MISC_FF = 0
MISC_GLR = FOX_HEADS
LOG2E = math.log2(math.e)


def _log_sigmoid(x):
    return jnp.minimum(x, 0.0) - jnp.log1p(jnp.exp(-jnp.abs(x)))


def _rms(x, g):
    return x * lax.rsqrt(jnp.mean(x * x, axis=-1, keepdims=True) + EPS) * g


def _cumsum(x, axis):
    n = x.shape[axis]
    idx = lax.broadcasted_iota(jnp.int32, x.shape, axis)
    k = 1
    while k < n:
        x = x + jnp.where(idx >= k, pltpu.roll(x, k, axis), 0.0)
        k *= 2
    return x


def _params(n_axes):
    return pltpu.CompilerParams(dimension_semantics=("arbitrary",) * n_axes,
                                vmem_limit_bytes=VMEM_LIMIT)


def _resident(shape):
    nd = len(shape)
    return pl.BlockSpec(shape, lambda *_: (0,) * nd, pipeline_mode=pl.Buffered(1))


def _proj_kernel(x_ref, g_ref, w_ref, *out_refs, groups, col_tile):
    tm = x_ref.shape[0]
    xb = _rms(x_ref[...], g_ref[...]).astype(BF16)
    for (c0, width, scale, outs) in groups:
        for s in range(0, width, col_tile):
            w = min(col_tile, width - s)
            z = jnp.dot(xb, w_ref[:, c0 + s:c0 + s + w], preferred_element_type=F32)
            if scale != 1.0:
                z = z * scale
            for oi, chunk in outs:
                ref = out_refs[oi]
                if chunk is None:
                    ref[:, s:s + w] = z.astype(ref.dtype)
                else:
                    zt = z.T
                    for j in range(tm // chunk):
                        ref[j, s:s + w, :] = zt[:, j * chunk:(j + 1) * chunk].astype(ref.dtype)


def _proj(x, g, w, tm, q_scale, q_chunk=None, v_chunk=None):
    n, d = x.shape
    widths = [FOX_WIDTH, FOX_WIDTH, FOX_WIDTH, LRU_WIDTH, LRU_WIDTH, GLA_QK_WIDTH, GLA_QK_WIDTH,
              GLA_V_WIDTH, GLA_V_WIDTH, N_BRANCH * d, MISC_WIDTH]
    offs = [sum(widths[:i]) for i in range(len(widths))]
    out_defs = [(FOX_WIDTH, BF16, q_chunk),
                (FOX_WIDTH, F32, None),
                (FOX_WIDTH, F32, None),
                (FOX_WIDTH, BF16, None),
                (FOX_WIDTH, BF16, v_chunk),
                (LRU_WIDTH, BF16, None),
                (LRU_WIDTH, BF16, None),
                (GLA_QK_WIDTH, BF16, None),
                (GLA_QK_WIDTH, BF16, None),
                (GLA_V_WIDTH, BF16, None),
                (GLA_V_WIDTH, BF16, None),
                (N_BRANCH * d, BF16, None),
                (MISC_WIDTH, F32, None)]
    groups = ((offs[0], widths[0], q_scale, ((0, q_chunk),)),
              (offs[1], widths[1], 1.0, ((1, None), (3, None))),
              (offs[2], widths[2], 1.0, ((2, None), (4, v_chunk))),
              (offs[3], widths[3], 1.0, ((5, None),)),
              (offs[4], widths[4], 1.0, ((6, None),)),
              (offs[5], widths[5], 1.0, ((7, None),)),
              (offs[6], widths[6], 1.0, ((8, None),)),
              (offs[7], widths[7], 1.0, ((9, None),)),
              (offs[8], widths[8], 1.0, ((10, None),)),
              (offs[9], widths[9], 1.0, ((11, None),)),
              (offs[10], widths[10], 1.0, ((12, None),)))

    def spec(wd, chunk):
        if chunk is None:
            return pl.BlockSpec((tm, wd), lambda i: (i, 0))
        return pl.BlockSpec((tm // chunk, wd, chunk), lambda i: (i, 0, 0))

    def shape(wd, dt, chunk):
        if chunk is None:
            return jax.ShapeDtypeStruct((n, wd), dt)
        return jax.ShapeDtypeStruct((n // chunk, wd, chunk), dt)

    return pl.pallas_call(
        functools.partial(_proj_kernel, groups=groups, col_tile=1024),
        grid=(n // tm,),
        in_specs=[pl.BlockSpec((tm, d), lambda i: (i, 0)),
                  _resident((1, d)),
                  _resident(w.shape)],
        out_specs=[spec(wd, ch) for wd, _, ch in out_defs],
        out_shape=[shape(wd, dt, ch) for wd, dt, ch in out_defs],
        compiler_params=_params(1),
        name="proj",
    )(x, g, w)


def _pack_w_in(w_in):
    sizes = (FOX_WIDTH, FOX_WIDTH, FOX_WIDTH, FOX_HEADS, LRU_WIDTH, LRU_WIDTH, GLA_QK_WIDTH, GLA_QK_WIDTH,
             GLA_V_WIDTH, GLA_RANK, GLA_V_WIDTH, N_BRANCH * w_in.shape[1])
    pts, acc = [], 0
    for s in sizes:
        pts.append((acc, acc + s))
        acc += s
    col = lambda i: w_in[:, :, pts[i][0]:pts[i][1]]
    pad = jnp.zeros(w_in.shape[:2] + (MISC_WIDTH - FOX_HEADS - GLA_RANK,), w_in.dtype)
    parts = [col(0), col(1), col(2), col(4), col(5), col(6), col(7), col(8), col(10), col(11),
             col(3), col(9), pad]
    return jnp.concatenate(parts, axis=-1).astype(BF16)


def _gate_kernel(misc_ref, fft_ref, b_ref, bt_ref, c0_ref, c0t_ref, logf_ref, c_ref, ct_ref, *, apply_ls, c_scale):
    z = misc_ref[...] + b_ref[...]
    zt = fft_ref[...] + bt_ref[...]
    if apply_ls:
        z = _log_sigmoid(z)
        zt = _log_sigmoid(zt)
    logf_ref[...] = z[:, :FOX_HEADS]
    c_ref[...] = ((_cumsum(z, 0) + c0_ref[...]) * c_scale)[:, :FOX_HEADS]
    ct_ref[...] = (_cumsum(zt, 1) + c0t_ref[...]) * c_scale


def _gate(misc, fft, b, bt, c0, c0t, apply_ls, c_scale=1.0):
    bsz, t, _ = misc.shape
    h = FOX_HEADS
    tl = fft.shape[2]
    return pl.pallas_call(
        functools.partial(_gate_kernel, apply_ls=apply_ls, c_scale=c_scale),
        grid=(bsz,),
        in_specs=[pl.BlockSpec((None, t, MISC_WIDTH), lambda i: (i, 0, 0)),
                  pl.BlockSpec((None, h, tl), lambda i: (i, 0, 0)),
                  pl.BlockSpec((1, MISC_WIDTH), lambda i: (0, 0)),
                  pl.BlockSpec((h, 1), lambda i: (0, 0)),
                  pl.BlockSpec((None, 1, MISC_WIDTH), lambda i: (i, 0, 0)),
                  pl.BlockSpec((None, h, 1), lambda i: (i, 0, 0))],
        out_specs=[pl.BlockSpec((None, t, h), lambda i: (i, 0, 0)),
                   pl.BlockSpec((None, t, h), lambda i: (i, 0, 0)),
                   pl.BlockSpec((None, h, tl), lambda i: (i, 0, 0))],
        out_shape=[jax.ShapeDtypeStruct((bsz, t, h), F32),
                   jax.ShapeDtypeStruct((bsz, t, h), F32),
                   jax.ShapeDtypeStruct((bsz, h, tl), F32)],
        compiler_params=_params(1),
        name="gate",
    )(misc, fft, b, bt, c0, c0t)


def _fox_kernel(qt_ref, k_ref, vt_ref, cqt_ref, ck_ref, o_ref, m_scr, l_scr, acc_scr, kb_scr, vb_scr, ckb_scr,
                s_scr, *, tq, tk):
    qi = pl.program_id(1)
    dh = FOX_HEAD_DIM
    lookahead = 2
    sub = tq // tk
    n_full = qi * sub
    kpos = lax.broadcasted_iota(jnp.int32, (tk, tq), 0)
    qpos = lax.broadcasted_iota(jnp.int32, (tk, tq), 1)
    m_scr[...] = jnp.full(m_scr.shape, -jnp.inf, F32)
    l_scr[...] = jnp.zeros(l_scr.shape, F32)
    acc_scr[...] = jnp.zeros(acc_scr.shape, F32)

    def block(ki, diag_j):
        r0 = pl.multiple_of(ki * tk, tk)
        kb_scr[...] = k_ref[pl.ds(r0, tk), :]
        vb_scr[...] = vt_ref[ki]
        ckb_scr[...] = ck_ref[pl.ds(r0, tk), :]
        def scores(h):
            hs = slice(h * dh, (h + 1) * dh)
            s_scr[h] = jnp.dot(kb_scr[:, hs], qt_ref[hs, :], preferred_element_type=F32)

        for h in range(lookahead):
            scores(h)
        for h in range(FOX_HEADS):
            hs = slice(h * dh, (h + 1) * dh)
            if h + lookahead < FOX_HEADS:
                scores(h + lookahead)
            s = s_scr[h] + (cqt_ref[h:h + 1, :] - ckb_scr[:, h:h + 1])
            if diag_j is not None:
                s = jnp.where(kpos + diag_j * tk <= qpos, s, -jnp.inf)
            m = m_scr[h]
            m_new = jnp.maximum(m, jnp.max(s, axis=0, keepdims=True))
            alpha = jnp.exp2(m - m_new)
            p = jnp.exp2(s - m_new)
            l_scr[h] = alpha * l_scr[h] + jnp.sum(p, axis=0, keepdims=True)
            m_scr[h] = m_new
            acc_scr[hs, :] = alpha * acc_scr[hs, :] + jnp.dot(vb_scr[hs, :], p.astype(BF16),
                                                              preferred_element_type=F32)

    def body(ki, carry):
        block(ki, None)
        return carry

    lax.fori_loop(0, n_full, body, 0)
    for j in range(sub):
        block(n_full + j, j)
    for h in range(FOX_HEADS):
        hs = slice(h * dh, (h + 1) * dh)
        o_ref[:, hs] = (acc_scr[hs, :] / l_scr[h]).T.astype(o_ref.dtype)


def _fox_prompt(qt, kb, vt, c, ct, bsz, t, tq, tk):
    nq = t // tq
    nk = t // tk
    w = FOX_WIDTH
    return pl.pallas_call(
        functools.partial(_fox_kernel, tq=tq, tk=tk),
        grid=(bsz, nq),
        in_specs=[pl.BlockSpec((None, w, tq), lambda b, i: (b * nq + i, 0, 0)),
                  pl.BlockSpec((t, w), lambda b, i: (b, 0)),
                  pl.BlockSpec((nk, w, tk), lambda b, i: (b, 0, 0)),
                  pl.BlockSpec((None, FOX_HEADS, tq), lambda b, i: (b, 0, i)),
                  pl.BlockSpec((t, FOX_HEADS), lambda b, i: (b, 0))],
        out_specs=pl.BlockSpec((tq, w), lambda b, i: (b * nq + i, 0)),
        out_shape=jax.ShapeDtypeStruct((bsz * t, w), BF16),
        scratch_shapes=[pltpu.VMEM((FOX_HEADS, 1, tq), F32),
                        pltpu.VMEM((FOX_HEADS, 1, tq), F32),
                        pltpu.VMEM((w, tq), F32),
                        pltpu.VMEM((tk, w), BF16),
                        pltpu.VMEM((w, tk), BF16),
                        pltpu.VMEM((tk, FOX_HEADS), F32),
                        pltpu.VMEM((FOX_HEADS, tk, tq), F32)],
        compiler_params=_params(2),
        name="fox_prompt",
    )(qt, kb, vt, ct, c)


def _fox_dec_kernel(q_ref, kn_ref, vn_ref, kc_ref, vc_ref, cq_ref, cnt_ref, cpt_ref, o_ref):
    dh = FOX_HEAD_DIM
    t = q_ref.shape[0]
    row = lax.broadcasted_iota(jnp.int32, (t, t), 0)
    col = lax.broadcasted_iota(jnp.int32, (t, t), 1)
    keep = col <= row
    nt = (((1,), (1,)), ((), ()))
    for h in range(FOX_HEADS):
        sl = slice(h * dh, (h + 1) * dh)
        q = q_ref[:, sl]
        cq = cq_ref[:, h:h + 1]
        sp = lax.dot_general(q, kc_ref[:, sl].astype(BF16), nt, preferred_element_type=F32)
        sp = sp + (cq - cpt_ref[h:h + 1, :])
        sn = lax.dot_general(q, kn_ref[:, sl], nt, preferred_element_type=F32)
        sn = sn + (cq - cnt_ref[h:h + 1, :t])
        sn = jnp.where(keep, sn, -jnp.inf)
        m = jnp.maximum(jnp.max(sp, axis=1, keepdims=True), jnp.max(sn, axis=1, keepdims=True))
        pp = jnp.exp(sp - m)
        pn = jnp.exp(sn - m)
        l = jnp.sum(pp, axis=1, keepdims=True) + jnp.sum(pn, axis=1, keepdims=True)
        o = (jnp.dot(pp.astype(BF16), vc_ref[:, sl].astype(BF16), preferred_element_type=F32)
             + jnp.dot(pn.astype(BF16), vn_ref[:, sl], preferred_element_type=F32))
        o_ref[:, sl] = (o / l).astype(o_ref.dtype)


def _fox_decode(q, kn, vn, kc, vc, cq, cnt, cpt):
    bsz, t, w = q.shape
    p = kc.shape[1]
    h = FOX_HEADS
    blk3 = lambda s1, s2: pl.BlockSpec((None, s1, s2), lambda b: (b, 0, 0))
    return pl.pallas_call(
        _fox_dec_kernel,
        grid=(bsz,),
        in_specs=[blk3(t, w), blk3(t, w), blk3(t, w), blk3(p, w), blk3(p, w),
                  blk3(t, h), blk3(h, cnt.shape[2]), blk3(h, p)],
        out_specs=blk3(t, w),
        out_shape=jax.ShapeDtypeStruct((bsz, t, w), BF16),
        compiler_params=_params(1),
        name="fox_decode",
    )(q, kn, vn, kc, vc, cq, cnt, cpt)


def _lru_kernel(lx_ref, lg_ref, buf_ref, h0_ref, cw_ref, cb_ref, wa_ref, ba_ref, wx_ref, bx_ref, lam_ref,
                y_ref, hl_ref, halo_scr, h_scr, a_scr, u_scr, *, tc, pitch):
    step = pl.program_id(0)
    bsz = lx_ref.shape[0]
    nblk = LRU_BLOCKS
    bd = LRU_BLOCK_DIM

    @pl.when(step == 0)
    def _init():
        halo_scr[...] = jnp.zeros(halo_scr.shape, F32)
        halo_scr[:, SUBLANES - (CONV_WIDTH - 1):, :] = buf_ref[...]
        h_scr[...] = h0_ref[...]

    log_lam = LRU_C * _log_sigmoid(lam_ref[...])
    for b in range(bsz):
        x = lx_ref[b].astype(F32)
        xp = jnp.concatenate([halo_scr[b], x], axis=0)
        xc = cb_ref[...] + cw_ref[CONV_WIDTH - 1:CONV_WIDTH, :] * x
        for j in range(1, CONV_WIDTH):
            shifted = pltpu.roll(xp, j, 0)[SUBLANES:, :]
            xc = xc + cw_ref[CONV_WIDTH - 1 - j:CONV_WIDTH - j, :] * shifted
        halo_scr[b] = x[tc - SUBLANES:, :] if tc >= SUBLANES else xp[tc:, :]
        for n in range(nblk):
            sl = slice(n * bd, (n + 1) * bd)
            xcn = xc[:, sl]
            xb = xcn.astype(BF16)
            r = jax.nn.sigmoid(jnp.dot(xb, wa_ref[n], preferred_element_type=F32) + ba_ref[:, sl])
            ig = jax.nn.sigmoid(jnp.dot(xb, wx_ref[n], preferred_element_type=F32) + bx_ref[:, sl])
            log_a = r * log_lam[:, sl]
            a = jnp.exp(log_a)
            mult = jnp.sqrt(-jnp.tanh(log_a) * (a * a + 1.0))
            a_scr[n, b * pitch:b * pitch + tc, :] = a
            u_scr[n, b * pitch:b * pitch + tc, :] = mult * ig * xcn

    def scan_body(t, hs):
        new = []
        for n in range(nblk):
            idx = pl.ds(t, bsz, stride=pitch)
            hn = a_scr[n, idx, :] * hs[n] + u_scr[n, idx, :]
            u_scr[n, idx, :] = hn
            new.append(hn)
        return tuple(new)

    h_init = tuple(h_scr[:, n * bd:(n + 1) * bd] for n in range(nblk))
    h_fin = lax.fori_loop(0, tc, scan_body, h_init)
    for n in range(nblk):
        h_scr[:, n * bd:(n + 1) * bd] = h_fin[n]
    hl_ref[...] = h_scr[...]

    for b in range(bsz):
        for n in range(nblk):
            sl = slice(n * bd, (n + 1) * bd)
            hseq = u_scr[n, b * pitch:b * pitch + tc, :]
            gate = jax.nn.gelu(lg_ref[b, :, sl].astype(F32), approximate=True)
            y_ref[b, :, sl] = (gate * hseq).astype(y_ref.dtype)


def _lru(lx, lg, buf, h0, cw, cb, wa, ba, wx, bx, lam, tc):
    bsz, t, w = lx.shape
    pitch = tc + 4
    rows = bsz * pitch
    rows = (rows + SUBLANES - 1) // SUBLANES * SUBLANES
    full = lambda shape: pl.BlockSpec(shape, lambda i: (0,) * len(shape))
    return pl.pallas_call(
        functools.partial(_lru_kernel, tc=tc, pitch=pitch),
        grid=(t // tc,),
        in_specs=[pl.BlockSpec((bsz, tc, w), lambda i: (0, i, 0)),
                  pl.BlockSpec((bsz, tc, w), lambda i: (0, i, 0)),
                  full(buf.shape), full(h0.shape), full(cw.shape), full(cb.shape),
                  full(wa.shape), full(ba.shape), full(wx.shape), full(bx.shape), full(lam.shape)],
        out_specs=[pl.BlockSpec((bsz, tc, w), lambda i: (0, i, 0)),
                   full((bsz, w))],
        out_shape=[jax.ShapeDtypeStruct((bsz, t, w), BF16),
                   jax.ShapeDtypeStruct((bsz, w), F32)],
        scratch_shapes=[pltpu.VMEM((bsz, SUBLANES, w), F32),
                        pltpu.VMEM((bsz, w), F32),
                        pltpu.VMEM((LRU_BLOCKS, rows, LRU_BLOCK_DIM), F32),
                        pltpu.VMEM((LRU_BLOCKS, rows, LRU_BLOCK_DIM), F32)],
        compiler_params=_params(1),
        name="lru",
    )(lx, lg, buf, h0, cw, cb, wa, ba, wx, bx, lam)


def _gla_kernel(gq_ref, gk_ref, gv_ref, gog_ref, misc_ref, wg_ref, bg_ref, gn_ref, s0_ref,
                y_ref, st_ref, *, chunk, n_sub):
    @pl.when(pl.program_id(1) == 0)
    def _init():
        st_ref[...] = s0_ref[...]

    dk, dv = GLA_DK, GLA_DV
    row = lax.broadcasted_iota(jnp.int32, (chunk, chunk), 0)
    col = lax.broadcasted_iota(jnp.int32, (chunk, chunk), 1)
    causal = col <= row
    nt = (((1,), (1,)), ((), ()))
    tn = (((0,), (0,)), ((), ()))
    for c in range(n_sub):
        rs = slice(c * chunk, (c + 1) * chunk)
        logit = jnp.dot(misc_ref[rs, :].astype(BF16), wg_ref[...], preferred_element_type=F32) + bg_ref[...]
        log_a = _log_sigmoid(logit) / GLA_TAU
        bcum = _cumsum(log_a, 0)
        b_last = bcum[chunk - 1:chunk, :]
        e_q = jnp.exp(bcum) * (dk ** -0.5)
        e_k = jnp.exp(-bcum)
        e_end = jnp.exp(b_last - bcum)
        decay = jnp.exp(b_last)
        qf = gq_ref[rs, :].astype(F32)
        kf = gk_ref[rs, :].astype(F32)
        for h in range(GLA_HEADS):
            ks = slice(h * dk, (h + 1) * dk)
            vs = slice(h * dv, (h + 1) * dv)
            q_t = (qf[:, ks] * e_q[:, ks]).astype(BF16)
            k_t = (kf[:, ks] * e_k[:, ks]).astype(BF16)
            k_e = (kf[:, ks] * e_end[:, ks]).astype(BF16)
            v = gv_ref[rs, vs]
            st = st_ref[h]
            a = lax.dot_general(q_t, k_t, nt, preferred_element_type=F32)
            a = jnp.where(causal, a, 0.0)
            o = (jnp.dot(a.astype(BF16), v, preferred_element_type=F32)
                 + lax.dot_general(q_t, st.astype(BF16), nt, preferred_element_type=F32))
            st_ref[h] = decay[:, ks] * st + lax.dot_general(v, k_e, tn, preferred_element_type=F32)
            on = _rms(o, gn_ref[...])
            g = gog_ref[rs, vs].astype(F32)
            y_ref[rs, vs] = (on * (g * jax.nn.sigmoid(g))).astype(y_ref.dtype)


def _gla(gq, gk, gv, gog, misc, wg, bg, gn, s0t, bsz, t, tcg, chunk):
    nc = t // tcg
    row_blk = lambda wd: pl.BlockSpec((tcg, wd), lambda b, i: (b * nc + i, 0))
    full = lambda shape: pl.BlockSpec(shape, lambda b, i: (0,) * len(shape))
    st_blk = pl.BlockSpec((None, GLA_HEADS, GLA_DV, GLA_DK), lambda b, i: (b, 0, 0, 0))
    return pl.pallas_call(
        functools.partial(_gla_kernel, chunk=chunk, n_sub=tcg // chunk),
        grid=(bsz, nc),
        in_specs=[row_blk(GLA_QK_WIDTH), row_blk(GLA_QK_WIDTH), row_blk(GLA_V_WIDTH), row_blk(GLA_V_WIDTH),
                  row_blk(MISC_WIDTH), full(wg.shape), full(bg.shape), full(gn.shape), st_blk],
        out_specs=[row_blk(GLA_V_WIDTH), st_blk],
        out_shape=[jax.ShapeDtypeStruct((bsz * t, GLA_V_WIDTH), BF16),
                   jax.ShapeDtypeStruct((bsz, GLA_HEADS, GLA_DV, GLA_DK), F32)],
        compiler_params=_params(2),
        name="gla",
    )(gq, gk, gv, gog, misc, wg, bg, gn, s0t)


def _mix_kernel(x_ref, yf_ref, yl_ref, yg_ref, mg_ref, wb_ref, wo_ref, wu_ref, wd_ref,
                g1_ref, g2_ref, g3_ref, o_ref):
    d = x_ref.shape[1]
    mix = None
    for i, y_ref in enumerate((yf_ref, yl_ref, yg_ref)):
        gate = jax.nn.sigmoid(mg_ref[:, i * d:(i + 1) * d].astype(F32))
        term = gate * jnp.dot(y_ref[...], wb_ref[i], preferred_element_type=F32)
        mix = term if mix is None else mix + term
    mo = jnp.dot(mix.astype(BF16), wo_ref[...], preferred_element_type=F32)
    x1 = x_ref[...] + _rms(mo, g1_ref[...])
    hn = _rms(x1, g2_ref[...]).astype(BF16)
    up = jnp.dot(hn, wu_ref[...], preferred_element_type=F32)
    act = jnp.square(jnp.maximum(up, 0.0)).astype(BF16)
    dn = jnp.dot(act, wd_ref[...], preferred_element_type=F32)
    o_ref[...] = x1 + _rms(dn, g3_ref[...])


def _mix(x, yf, yl, yg, mg, wb, wo, wu, wd, g1, g2, g3, tm):
    n, d = x.shape
    row = lambda wd_: pl.BlockSpec((tm, wd_), lambda i: (i, 0))
    return pl.pallas_call(
        _mix_kernel,
        grid=(n // tm,),
        in_specs=[row(d), row(d), row(d), row(d), row(N_BRANCH * d),
                  _resident(wb.shape), _resident(wo.shape), _resident(wu.shape), _resident(wd.shape),
                  _resident(g1.shape), _resident(g2.shape), _resident(g3.shape)],
        out_specs=row(d),
        out_shape=jax.ShapeDtypeStruct((n, d), F32),
        compiler_params=_params(1),
        name="mix",
    )(x, yf, yl, yg, mg, wb, wo, wu, wd, g1, g2, g3)


def _layer(x, p, past, bsz, t):
    d = x.shape[1]
    prompt = past is None
    tm = min(256, bsz * t)
    if prompt:
        tq = tk = min(256, t)
        q, k, v, kb, vb, lx, lg, gq, gk, gv, gog, mg, misc = _proj(
            x, p['g_pre'], p['w_in'], tm, FOX_HEAD_DIM ** -0.5 * LOG2E, q_chunk=tq, v_chunk=tk)
    else:
        q, k, v, kb, vb, lx, lg, gq, gk, gv, gog, mg, misc = _proj(
            x, p['g_pre'], p['w_in'], tm, FOX_HEAD_DIM ** -0.5)

    misc3 = misc.reshape(bsz, t, MISC_WIDTH)
    fft = jnp.swapaxes(misc3[:, :, MISC_FF:MISC_FF + FOX_HEADS], 1, 2)
    if prompt:
        zc = jnp.zeros((bsz, 1, MISC_WIDTH), F32)
        zct = jnp.zeros((bsz, FOX_HEADS, 1), F32)
        logf, c, ct = _gate(misc3, fft, p['b_ff'], p['b_ff_t'], zc, zct, True, c_scale=LOG2E)
        y_fox = _fox_prompt(q, kb, vb, c.reshape(bsz * t, FOX_HEADS), ct, bsz, t, tq, tk)
        conv_buf = jnp.zeros((bsz, CONV_WIDTH - 1, LRU_WIDTH), F32)
        h0 = jnp.zeros((bsz, LRU_WIDTH), F32)
        s0t = jnp.zeros((bsz, GLA_HEADS, GLA_DV, GLA_DK), F32)
    else:
        k_c, v_c, lf_c, conv_buf, h0, s0 = past
        pl_len = k_c.shape[1]
        k_c = k_c.reshape(bsz, pl_len, FOX_WIDTH)
        v_c = v_c.reshape(bsz, pl_len, FOX_WIDTH)
        lf_pad = jnp.pad(lf_c, ((0, 0), (0, 0), (0, MISC_WIDTH - FOX_HEADS)))
        zb = jnp.zeros((1, MISC_WIDTH), F32)
        zbt = jnp.zeros((FOX_HEADS, 1), F32)
        zc = jnp.zeros((bsz, 1, MISC_WIDTH), F32)
        zct = jnp.zeros((bsz, FOX_HEADS, 1), F32)
        _, c_past, cpt = _gate(lf_pad, jnp.swapaxes(lf_c, 1, 2), zb, zbt, zc, zct, False)
        tot = jnp.pad(c_past[:, pl_len - 1:pl_len, :], ((0, 0), (0, 0), (0, MISC_WIDTH - FOX_HEADS)))
        tot_t = cpt[:, :, pl_len - 1:pl_len]
        fft_pad = jnp.pad(fft, ((0, 0), (0, 0), (0, LANES - t)))
        logf, c, cnt = _gate(misc3, fft_pad, p['b_ff'], p['b_ff_t'], tot, tot_t, True)
        y_fox = _fox_decode(q.reshape(bsz, t, FOX_WIDTH), kb.reshape(bsz, t, FOX_WIDTH),
                            vb.reshape(bsz, t, FOX_WIDTH), k_c, v_c, c, cnt, cpt)
        y_fox = y_fox.reshape(bsz * t, FOX_WIDTH)
        s0t = jnp.swapaxes(s0, 2, 3)

    lx3 = lx.reshape(bsz, t, LRU_WIDTH)
    y_lru, h_last = _lru(lx3, lg.reshape(bsz, t, LRU_WIDTH), conv_buf, h0, p['conv_w'], p['conv_b'],
                         p['w_a'], p['b_a'], p['w_x'], p['b_x'], p['lam'], min(128, t))
    assert t >= CONV_WIDTH - 1
    conv_new = lx3[:, t - (CONV_WIDTH - 1):].astype(F32)

    chunk = min(CHUNK, t)
    y_gla, s_t = _gla(gq, gk, gv, gog, misc, p['w_gate'], p['b_gate'], p['gla_norm'], s0t,
                      bsz, t, min(256, t), chunk)
    s_new = jnp.swapaxes(s_t, 2, 3)

    x = _mix(x, y_fox, y_lru.reshape(bsz * t, LRU_WIDTH), y_gla, mg, p['w_branch'], p['w_out'],
             p['w_up'], p['w_down'], p['g_post'], p['g_mlp_pre'], p['g_mlp_post'], tm)
    new_state = (k.reshape(bsz, t, FOX_HEADS, FOX_HEAD_DIM), v.reshape(bsz, t, FOX_HEADS, FOX_HEAD_DIM),
                 logf, conv_new, h_last, s_new)
    return x, new_state


def kernel(x_prompt, x_sample, cache_fox_k, cache_fox_v, cache_fox_logf, state_lru_conv, state_lru_h, state_gla, norm_mix_pre, norm_mix_post, w_in, b_fox_f, lru_conv_w, lru_conv_b, lru_w_a, lru_b_a, lru_w_x, lru_b_x, lru_lambda, gla_w_gate_up, gla_b_gate, gla_norm, w_branch, w_out, norm_mlp_pre, norm_mlp_post, w_up, w_down):
    depth, d = norm_mix_pre.shape
    bp, tp, _ = x_prompt.shape
    bs, ts, _ = x_sample.shape

    w_in_p = _pack_w_in(w_in)
    row = lambda a: a.reshape(depth, 1, a.shape[-1])
    b_ff = jnp.pad(b_fox_f, ((0, 0), (0, MISC_WIDTH - FOX_HEADS))).reshape(depth, 1, MISC_WIDTH)
    b_ff_t = b_fox_f.reshape(depth, FOX_HEADS, 1)
    w_gate = jnp.zeros((depth, MISC_WIDTH, GLA_QK_WIDTH), F32)
    w_gate = w_gate.at[:, MISC_GLR:MISC_GLR + GLA_RANK, :].set(gla_w_gate_up).astype(BF16)
    layers = []
    for l in range(depth):
        layers.append({
            'g_pre': row(norm_mix_pre)[l], 'w_in': w_in_p[l],
            'b_ff': b_ff[l], 'b_ff_t': b_ff_t[l],
            'conv_w': lru_conv_w[l], 'conv_b': row(lru_conv_b)[l],
            'w_a': lru_w_a[l].astype(BF16), 'b_a': row(lru_b_a)[l],
            'w_x': lru_w_x[l].astype(BF16), 'b_x': row(lru_b_x)[l], 'lam': row(lru_lambda)[l],
            'w_gate': w_gate[l], 'b_gate': row(gla_b_gate)[l], 'gla_norm': row(gla_norm)[l],
            'w_branch': w_branch[l].astype(BF16), 'w_out': w_out[l].astype(BF16),
            'w_up': w_up[l].astype(BF16), 'w_down': w_down[l].astype(BF16),
            'g_post': row(norm_mix_post)[l], 'g_mlp_pre': row(norm_mlp_pre)[l],
            'g_mlp_post': row(norm_mlp_post)[l],
        })

    hp = x_prompt.reshape(bp * tp, d)
    hs = x_sample.reshape(bs * ts, d)
    new_p = [[] for _ in range(6)]
    new_s = [[] for _ in range(6)]
    for l in range(depth):
        hp, st_p = _layer(hp, layers[l], None, bp, tp)
        past = (cache_fox_k[l], cache_fox_v[l], cache_fox_logf[l], state_lru_conv[l], state_lru_h[l],
                state_gla[l])
        hs, st_s = _layer(hs, layers[l], past, bs, ts)
        for lst, a in zip(new_p, st_p):
            lst.append(a)
        for lst, a in zip(new_s, st_s):
            lst.append(a)
    outs_p = [jnp.stack(a) for a in new_p]
    outs_s = [jnp.stack(a) for a in new_s]
    return (hp.reshape(bp, tp, d), hs.reshape(bs, ts, d), *outs_p, *outs_s)
```

```python
import functools
import math

import jax
import jax.numpy as jnp
from jax import lax
from jax.experimental import pallas as pl
from jax.experimental.pallas import tpu as pltpu

F32 = jnp.float32
BF16 = jnp.bfloat16

EPS = 1e-6
CHUNK = 64
FOX_HEADS = 8
FOX_HEAD_DIM = 128
FOX_WIDTH = FOX_HEADS * FOX_HEAD_DIM
LRU_WIDTH = 1024
LRU_BLOCKS = 8
LRU_BLOCK_DIM = LRU_WIDTH // LRU_BLOCKS
CONV_WIDTH = 4
LRU_C = 8.0
GLA_HEADS = 4
GLA_DK = 128
GLA_DV = 256
GLA_QK_WIDTH = GLA_HEADS * GLA_DK
GLA_V_WIDTH = GLA_HEADS * GLA_DV
GLA_RANK = 16
GLA_TAU = 16.0
N_BRANCH = 3
LANES = 128
SUBLANES = 8
VMEM_LIMIT = 56 * 1024 * 1024

MISC_WIDTH = LANES
MISC_FF = 0
MISC_GLR = FOX_HEADS
LOG2E = math.log2(math.e)


def _log_sigmoid(x):
    return jnp.minimum(x, 0.0) - jnp.log1p(jnp.exp(-jnp.abs(x)))


def _rms(x, g):
    return x * lax.rsqrt(jnp.mean(x * x, axis=-1, keepdims=True) + EPS) * g


def _cumsum(x, axis):
    n = x.shape[axis]
    idx = lax.broadcasted_iota(jnp.int32, x.shape, axis)
    k = 1
    while k < n:
        x = x + jnp.where(idx >= k, pltpu.roll(x, k, axis), 0.0)
        k *= 2
    return x


def _params(n_axes):
    return pltpu.CompilerParams(dimension_semantics=("arbitrary",) * n_axes,
                                vmem_limit_bytes=VMEM_LIMIT)


def _resident(shape):
    nd = len(shape)
    return pl.BlockSpec(shape, lambda *_: (0,) * nd, pipeline_mode=pl.Buffered(1))


def _proj_kernel(x_ref, g_ref, w_ref, *out_refs, groups, col_tile):
    tm = x_ref.shape[0]
    xb = _rms(x_ref[...], g_ref[...]).astype(BF16)
    for (c0, width, scale, outs) in groups:
        for s in range(0, width, col_tile):
            w = min(col_tile, width - s)
            z = jnp.dot(xb, w_ref[:, c0 + s:c0 + s + w], preferred_element_type=F32)
            if scale != 1.0:
                z = z * scale
            for oi, chunk in outs:
                ref = out_refs[oi]
                if chunk is None:
                    ref[:, s:s + w] = z.astype(ref.dtype)
                else:
                    zt = z.T
                    for j in range(tm // chunk):
                        ref[j, s:s + w, :] = zt[:, j * chunk:(j + 1) * chunk].astype(ref.dtype)


def _proj(x, g, w, tm, q_scale, q_chunk=None, v_chunk=None):
    n, d = x.shape
    widths = [FOX_WIDTH, FOX_WIDTH, FOX_WIDTH, LRU_WIDTH, LRU_WIDTH, GLA_QK_WIDTH, GLA_QK_WIDTH,
              GLA_V_WIDTH, GLA_V_WIDTH, N_BRANCH * d, MISC_WIDTH]
    offs = [sum(widths[:i]) for i in range(len(widths))]
    out_defs = [(FOX_WIDTH, BF16, q_chunk),
                (FOX_WIDTH, F32, None),
                (FOX_WIDTH, F32, None),
                (FOX_WIDTH, BF16, None),
                (FOX_WIDTH, BF16, v_chunk),
                (LRU_WIDTH, BF16, None),
                (LRU_WIDTH, BF16, None),
                (GLA_QK_WIDTH, BF16, None),
                (GLA_QK_WIDTH, BF16, None),
                (GLA_V_WIDTH, BF16, None),
                (GLA_V_WIDTH, BF16, None),
                (N_BRANCH * d, BF16, None),
                (MISC_WIDTH, F32, None)]
    groups = ((offs[0], widths[0], q_scale, ((0, q_chunk),)),
              (offs[1], widths[1], 1.0, ((1, None), (3, None))),
              (offs[2], widths[2], 1.0, ((2, None), (4, v_chunk))),
              (offs[3], widths[3], 1.0, ((5, None),)),
              (offs[4], widths[4], 1.0, ((6, None),)),
              (offs[5], widths[5], 1.0, ((7, None),)),
              (offs[6], widths[6], 1.0, ((8, None),)),
              (offs[7], widths[7], 1.0, ((9, None),)),
              (offs[8], widths[8], 1.0, ((10, None),)),
              (offs[9], widths[9], 1.0, ((11, None),)),
              (offs[10], widths[10], 1.0, ((12, None),)))

    def spec(wd, chunk):
        if chunk is None:
            return pl.BlockSpec((tm, wd), lambda i: (i, 0))
        return pl.BlockSpec((tm // chunk, wd, chunk), lambda i: (i, 0, 0))

    def shape(wd, dt, chunk):
        if chunk is None:
            return jax.ShapeDtypeStruct((n, wd), dt)
        return jax.ShapeDtypeStruct((n // chunk, wd, chunk), dt)

    return pl.pallas_call(
        functools.partial(_proj_kernel, groups=groups, col_tile=1024),
        grid=(n // tm,),
        in_specs=[pl.BlockSpec((tm, d), lambda i: (i, 0)),
                  _resident((1, d)),
                  _resident(w.shape)],
        out_specs=[spec(wd, ch) for wd, _, ch in out_defs],
        out_shape=[shape(wd, dt, ch) for wd, dt, ch in out_defs],
        compiler_params=_params(1),
        name="proj",
    )(x, g, w)


def _pack_w_in(w_in):
    sizes = (FOX_WIDTH, FOX_WIDTH, FOX_WIDTH, FOX_HEADS, LRU_WIDTH, LRU_WIDTH, GLA_QK_WIDTH, GLA_QK_WIDTH,
             GLA_V_WIDTH, GLA_RANK, GLA_V_WIDTH, N_BRANCH * w_in.shape[1])
    pts, acc = [], 0
    for s in sizes:
        pts.append((acc, acc + s))
        acc += s
    col = lambda i: w_in[:, :, pts[i][0]:pts[i][1]]
    pad = jnp.zeros(w_in.shape[:2] + (MISC_WIDTH - FOX_HEADS - GLA_RANK,), w_in.dtype)
    parts = [col(0), col(1), col(2), col(4), col(5), col(6), col(7), col(8), col(10), col(11),
             col(3), col(9), pad]
    return jnp.concatenate(parts, axis=-1).astype(BF16)


def _gate_kernel(misc_ref, fft_ref, b_ref, bt_ref, c0_ref, c0t_ref, logf_ref, c_ref, ct_ref, *, apply_ls, c_scale):
    z = misc_ref[...] + b_ref[...]
    zt = fft_ref[...] + bt_ref[...]
    if apply_ls:
        z = _log_sigmoid(z)
        zt = _log_sigmoid(zt)
    logf_ref[...] = z[:, :FOX_HEADS]
    c_ref[...] = ((_cumsum(z, 0) + c0_ref[...]) * c_scale)[:, :FOX_HEADS]
    ct_ref[...] = (_cumsum(zt, 1) + c0t_ref[...]) * c_scale


def _gate(misc, fft, b, bt, c0, c0t, apply_ls, c_scale=1.0):
    bsz, t, _ = misc.shape
    h = FOX_HEADS
    tl = fft.shape[2]
    return pl.pallas_call(
        functools.partial(_gate_kernel, apply_ls=apply_ls, c_scale=c_scale),
        grid=(bsz,),
        in_specs=[pl.BlockSpec((None, t, MISC_WIDTH), lambda i: (i, 0, 0)),
                  pl.BlockSpec((None, h, tl), lambda i: (i, 0, 0)),
                  pl.BlockSpec((1, MISC_WIDTH), lambda i: (0, 0)),
                  pl.BlockSpec((h, 1), lambda i: (0, 0)),
                  pl.BlockSpec((None, 1, MISC_WIDTH), lambda i: (i, 0, 0)),
                  pl.BlockSpec((None, h, 1), lambda i: (i, 0, 0))],
        out_specs=[pl.BlockSpec((None, t, h), lambda i: (i, 0, 0)),
                   pl.BlockSpec((None, t, h), lambda i: (i, 0, 0)),
                   pl.BlockSpec((None, h, tl), lambda i: (i, 0, 0))],
        out_shape=[jax.ShapeDtypeStruct((bsz, t, h), F32),
                   jax.ShapeDtypeStruct((bsz, t, h), F32),
                   jax.ShapeDtypeStruct((bsz, h, tl), F32)],
        compiler_params=_params(1),
        name="gate",
    )(misc, fft, b, bt, c0, c0t)


def _fox_kernel(qt_ref, k_ref, vt_ref, cqt_ref, ck_ref, o_ref, m_scr, l_scr, acc_scr, kb_scr, vb_scr, ckb_scr,
                s_scr, *, tq, tk):
    qi = pl.program_id(1)
    dh = FOX_HEAD_DIM
    lookahead = 2
    sub = tq // tk
    n_full = qi * sub
    kpos = lax.broadcasted_iota(jnp.int32, (tk, tq), 0)
    qpos = lax.broadcasted_iota(jnp.int32, (tk, tq), 1)
    m_scr[...] = jnp.full(m_scr.shape, -jnp.inf, F32)
    l_scr[...] = jnp.zeros(l_scr.shape, F32)
    acc_scr[...] = jnp.zeros(acc_scr.shape, F32)

    def block(ki, diag_j):
        r0 = pl.multiple_of(ki * tk, tk)
        kb_scr[...] = k_ref[pl.ds(r0, tk), :]
        vb_scr[...] = vt_ref[ki]
        ckb_scr[...] = ck_ref[pl.ds(r0, tk), :]
        def scores(h):
            hs = slice(h * dh, (h + 1) * dh)
            s_scr[h] = jnp.dot(kb_scr[:, hs], qt_ref[hs, :], preferred_element_type=F32)

        for h in range(lookahead):
            scores(h)
        for h in range(FOX_HEADS):
            hs = slice(h * dh, (h + 1) * dh)
            if h + lookahead < FOX_HEADS:
                scores(h + lookahead)
            s = s_scr[h] + (cqt_ref[h:h + 1, :] - ckb_scr[:, h:h + 1])
            if diag_j is not None:
                s = jnp.where(kpos + diag_j * tk <= qpos, s, -jnp.inf)
            m = m_scr[h]
            m_new = jnp.maximum(m, jnp.max(s, axis=0, keepdims=True))
            alpha = jnp.exp2(m - m_new)
            p = jnp.exp2(s - m_new)
            l_scr[h] = alpha * l_scr[h] + jnp.sum(p, axis=0, keepdims=True)
            m_scr[h] = m_new
            acc_scr[hs, :] = alpha * acc_scr[hs, :] + jnp.dot(vb_scr[hs, :], p.astype(BF16),
                                                              preferred_element_type=F32)

    def body(ki, carry):
        block(ki, None)
        return carry

    lax.fori_loop(0, n_full, body, 0)
    for j in range(sub):
        block(n_full + j, j)
    for h in range(FOX_HEADS):
        hs = slice(h * dh, (h + 1) * dh)
        o_ref[:, hs] = (acc_scr[hs, :] / l_scr[h]).T.astype(o_ref.dtype)


def _fox_prompt(qt, kb, vt, c, ct, bsz, t, tq, tk):
    nq = t // tq
    nk = t // tk
    w = FOX_WIDTH
    return pl.pallas_call(
        functools.partial(_fox_kernel, tq=tq, tk=tk),
        grid=(bsz, nq),
        in_specs=[pl.BlockSpec((None, w, tq), lambda b, i: (b * nq + i, 0, 0)),
                  pl.BlockSpec((t, w), lambda b, i: (b, 0)),
                  pl.BlockSpec((nk, w, tk), lambda b, i: (b, 0, 0)),
                  pl.BlockSpec((None, FOX_HEADS, tq), lambda b, i: (b, 0, i)),
                  pl.BlockSpec((t, FOX_HEADS), lambda b, i: (b, 0))],
        out_specs=pl.BlockSpec((tq, w), lambda b, i: (b * nq + i, 0)),
        out_shape=jax.ShapeDtypeStruct((bsz * t, w), BF16),
        scratch_shapes=[pltpu.VMEM((FOX_HEADS, 1, tq), F32),
                        pltpu.VMEM((FOX_HEADS, 1, tq), F32),
                        pltpu.VMEM((w, tq), F32),
                        pltpu.VMEM((tk, w), BF16),
                        pltpu.VMEM((w, tk), BF16),
                        pltpu.VMEM((tk, FOX_HEADS), F32),
                        pltpu.VMEM((FOX_HEADS, tk, tq), F32)],
        compiler_params=_params(2),
        name="fox_prompt",
    )(qt, kb, vt, ct, c)


def _fox_dec_kernel(q_ref, kn_ref, vn_ref, kc_ref, vc_ref, cq_ref, cnt_ref, cpt_ref, o_ref):
    dh = FOX_HEAD_DIM
    t = q_ref.shape[0]
    row = lax.broadcasted_iota(jnp.int32, (t, t), 0)
    col = lax.broadcasted_iota(jnp.int32, (t, t), 1)
    keep = col <= row
    nt = (((1,), (1,)), ((), ()))
    for h in range(FOX_HEADS):
        sl = slice(h * dh, (h + 1) * dh)
        q = q_ref[:, sl]
        cq = cq_ref[:, h:h + 1]
        sp = lax.dot_general(q, kc_ref[:, sl].astype(BF16), nt, preferred_element_type=F32)
        sp = sp + (cq - cpt_ref[h:h + 1, :])
        sn = lax.dot_general(q, kn_ref[:, sl], nt, preferred_element_type=F32)
        sn = sn + (cq - cnt_ref[h:h + 1, :t])
        sn = jnp.where(keep, sn, -jnp.inf)
        m = jnp.maximum(jnp.max(sp, axis=1, keepdims=True), jnp.max(sn, axis=1, keepdims=True))
        pp = jnp.exp(sp - m)
        pn = jnp.exp(sn - m)
        l = jnp.sum(pp, axis=1, keepdims=True) + jnp.sum(pn, axis=1, keepdims=True)
        o = (jnp.dot(pp.astype(BF16), vc_ref[:, sl].astype(BF16), preferred_element_type=F32)
             + jnp.dot(pn.astype(BF16), vn_ref[:, sl], preferred_element_type=F32))
        o_ref[:, sl] = (o / l).astype(o_ref.dtype)


def _fox_decode(q, kn, vn, kc, vc, layer, cq, cnt, cpt):
    bsz, t, w = q.shape
    p = kc.shape[2]
    h = FOX_HEADS
    blk3 = lambda s1, s2: pl.BlockSpec((None, s1, s2), lambda b: (b, 0, 0))
    cache = pl.BlockSpec((None, None, p, w), lambda b: (layer, b, 0, 0))
    return pl.pallas_call(
        _fox_dec_kernel,
        grid=(bsz,),
        in_specs=[blk3(t, w), blk3(t, w), blk3(t, w), cache, cache,
                  blk3(t, h), blk3(h, cnt.shape[2]), blk3(h, p)],
        out_specs=blk3(t, w),
        out_shape=jax.ShapeDtypeStruct((bsz, t, w), BF16),
        compiler_params=_params(1),
        name="fox_decode",
    )(q, kn, vn, kc, vc, cq, cnt, cpt)


def _lru_kernel(lx_ref, lg_ref, buf_ref, h0_ref, cw_ref, cb_ref, wa_ref, ba_ref, wx_ref, bx_ref, lam_ref,
                y_ref, hl_ref, halo_scr, h_scr, a_scr, u_scr, *, tc, pitch):
    step = pl.program_id(0)
    bsz = lx_ref.shape[0]
    nblk = LRU_BLOCKS
    bd = LRU_BLOCK_DIM

    @pl.when(step == 0)
    def _init():
        halo_scr[...] = jnp.zeros(halo_scr.shape, F32)
        halo_scr[:, SUBLANES - (CONV_WIDTH - 1):, :] = buf_ref[...]
        h_scr[...] = h0_ref[...]

    log_lam = LRU_C * _log_sigmoid(lam_ref[...])
    for b in range(bsz):
        x = lx_ref[b].astype(F32)
        xp = jnp.concatenate([halo_scr[b], x], axis=0)
        xc = cb_ref[...] + cw_ref[CONV_WIDTH - 1:CONV_WIDTH, :] * x
        for j in range(1, CONV_WIDTH):
            shifted = pltpu.roll(xp, j, 0)[SUBLANES:, :]
            xc = xc + cw_ref[CONV_WIDTH - 1 - j:CONV_WIDTH - j, :] * shifted
        halo_scr[b] = x[tc - SUBLANES:, :] if tc >= SUBLANES else xp[tc:, :]
        for n in range(nblk):
            sl = slice(n * bd, (n + 1) * bd)
            xcn = xc[:, sl]
            xb = xcn.astype(BF16)
            r = jax.nn.sigmoid(jnp.dot(xb, wa_ref[n], preferred_element_type=F32) + ba_ref[:, sl])
            ig = jax.nn.sigmoid(jnp.dot(xb, wx_ref[n], preferred_element_type=F32) + bx_ref[:, sl])
            log_a = r * log_lam[:, sl]
            a = jnp.exp(log_a)
            mult = jnp.sqrt(-jnp.tanh(log_a) * (a * a + 1.0))
            a_scr[n, b * pitch:b * pitch + tc, :] = a
            u_scr[n, b * pitch:b * pitch + tc, :] = mult * ig * xcn

    def scan_body(t, hs):
        new = []
        for n in range(nblk):
            idx = pl.ds(t, bsz, stride=pitch)
            hn = a_scr[n, idx, :] * hs[n] + u_scr[n, idx, :]
            u_scr[n, idx, :] = hn
            new.append(hn)
        return tuple(new)

    h_init = tuple(h_scr[:, n * bd:(n + 1) * bd] for n in range(nblk))
    h_fin = lax.fori_loop(0, tc, scan_body, h_init)
    for n in range(nblk):
        h_scr[:, n * bd:(n + 1) * bd] = h_fin[n]
    hl_ref[...] = h_scr[...]

    for b in range(bsz):
        for n in range(nblk):
            sl = slice(n * bd, (n + 1) * bd)
            hseq = u_scr[n, b * pitch:b * pitch + tc, :]
            gate = jax.nn.gelu(lg_ref[b, :, sl].astype(F32), approximate=True)
            y_ref[b, :, sl] = (gate * hseq).astype(y_ref.dtype)


def _lru(lx, lg, buf, h0, cw, cb, wa, ba, wx, bx, lam, tc):
    bsz, t, w = lx.shape
    pitch = tc + 4
    rows = bsz * pitch
    rows = (rows + SUBLANES - 1) // SUBLANES * SUBLANES
    full = lambda shape: pl.BlockSpec(shape, lambda i: (0,) * len(shape))
    return pl.pallas_call(
        functools.partial(_lru_kernel, tc=tc, pitch=pitch),
        grid=(t // tc,),
        in_specs=[pl.BlockSpec((bsz, tc, w), lambda i: (0, i, 0)),
                  pl.BlockSpec((bsz, tc, w), lambda i: (0, i, 0)),
                  full(buf.shape), full(h0.shape), full(cw.shape), full(cb.shape),
                  full(wa.shape), full(ba.shape), full(wx.shape), full(bx.shape), full(lam.shape)],
        out_specs=[pl.BlockSpec((bsz, tc, w), lambda i: (0, i, 0)),
                   full((bsz, w))],
        out_shape=[jax.ShapeDtypeStruct((bsz, t, w), BF16),
                   jax.ShapeDtypeStruct((bsz, w), F32)],
        scratch_shapes=[pltpu.VMEM((bsz, SUBLANES, w), F32),
                        pltpu.VMEM((bsz, w), F32),
                        pltpu.VMEM((LRU_BLOCKS, rows, LRU_BLOCK_DIM), F32),
                        pltpu.VMEM((LRU_BLOCKS, rows, LRU_BLOCK_DIM), F32)],
        compiler_params=_params(1),
        name="lru",
    )(lx, lg, buf, h0, cw, cb, wa, ba, wx, bx, lam)


def _gla_kernel(gq_ref, gk_ref, gv_ref, gog_ref, misc_ref, wg_ref, bg_ref, gn_ref, s0_ref,
                y_ref, st_ref, *, chunk, n_sub):
    @pl.when(pl.program_id(1) == 0)
    def _init():
        st_ref[...] = s0_ref[...]

    dk, dv = GLA_DK, GLA_DV
    row = lax.broadcasted_iota(jnp.int32, (chunk, chunk), 0)
    col = lax.broadcasted_iota(jnp.int32, (chunk, chunk), 1)
    causal = col <= row
    nt = (((1,), (1,)), ((), ()))
    tn = (((0,), (0,)), ((), ()))
    for c in range(n_sub):
        rs = slice(c * chunk, (c + 1) * chunk)
        logit = jnp.dot(misc_ref[rs, :].astype(BF16), wg_ref[...], preferred_element_type=F32) + bg_ref[...]
        log_a = _log_sigmoid(logit) / GLA_TAU
        bcum = _cumsum(log_a, 0)
        b_last = bcum[chunk - 1:chunk, :]
        e_q = jnp.exp(bcum) * (dk ** -0.5)
        e_k = jnp.exp(-bcum)
        e_end = jnp.exp(b_last - bcum)
        decay = jnp.exp(b_last)
        qf = gq_ref[rs, :].astype(F32)
        kf = gk_ref[rs, :].astype(F32)
        for h in range(GLA_HEADS):
            ks = slice(h * dk, (h + 1) * dk)
            vs = slice(h * dv, (h + 1) * dv)
            q_t = (qf[:, ks] * e_q[:, ks]).astype(BF16)
            k_t = (kf[:, ks] * e_k[:, ks]).astype(BF16)
            k_e = (kf[:, ks] * e_end[:, ks]).astype(BF16)
            v = gv_ref[rs, vs]
            st = st_ref[h]
            a = lax.dot_general(q_t, k_t, nt, preferred_element_type=F32)
            a = jnp.where(causal, a, 0.0)
            o = (jnp.dot(a.astype(BF16), v, preferred_element_type=F32)
                 + lax.dot_general(q_t, st.astype(BF16), nt, preferred_element_type=F32))
            st_ref[h] = decay[:, ks] * st + lax.dot_general(v, k_e, tn, preferred_element_type=F32)
            on = _rms(o, gn_ref[...])
            g = gog_ref[rs, vs].astype(F32)
            y_ref[rs, vs] = (on * (g * jax.nn.sigmoid(g))).astype(y_ref.dtype)


def _gla(gq, gk, gv, gog, misc, wg, bg, gn, s0t, bsz, t, tcg, chunk):
    nc = t // tcg
    row_blk = lambda wd: pl.BlockSpec((tcg, wd), lambda b, i: (b * nc + i, 0))
    full = lambda shape: pl.BlockSpec(shape, lambda b, i: (0,) * len(shape))
    st_blk = pl.BlockSpec((None, GLA_HEADS, GLA_DV, GLA_DK), lambda b, i: (b, 0, 0, 0))
    return pl.pallas_call(
        functools.partial(_gla_kernel, chunk=chunk, n_sub=tcg // chunk),
        grid=(bsz, nc),
        in_specs=[row_blk(GLA_QK_WIDTH), row_blk(GLA_QK_WIDTH), row_blk(GLA_V_WIDTH), row_blk(GLA_V_WIDTH),
                  row_blk(MISC_WIDTH), full(wg.shape), full(bg.shape), full(gn.shape), st_blk],
        out_specs=[row_blk(GLA_V_WIDTH), st_blk],
        out_shape=[jax.ShapeDtypeStruct((bsz * t, GLA_V_WIDTH), BF16),
                   jax.ShapeDtypeStruct((bsz, GLA_HEADS, GLA_DV, GLA_DK), F32)],
        compiler_params=_params(2),
        name="gla",
    )(gq, gk, gv, gog, misc, wg, bg, gn, s0t)


def _mix_kernel(x_ref, yf_ref, yl_ref, yg_ref, mg_ref, wb_ref, wo_ref, wu_ref, wd_ref,
                g1_ref, g2_ref, g3_ref, o_ref):
    d = x_ref.shape[1]
    mix = None
    for i, y_ref in enumerate((yf_ref, yl_ref, yg_ref)):
        gate = jax.nn.sigmoid(mg_ref[:, i * d:(i + 1) * d].astype(F32))
        term = gate * jnp.dot(y_ref[...], wb_ref[i], preferred_element_type=F32)
        mix = term if mix is None else mix + term
    mo = jnp.dot(mix.astype(BF16), wo_ref[...], preferred_element_type=F32)
    x1 = x_ref[...] + _rms(mo, g1_ref[...])
    hn = _rms(x1, g2_ref[...]).astype(BF16)
    up = jnp.dot(hn, wu_ref[...], preferred_element_type=F32)
    act = jnp.square(jnp.maximum(up, 0.0)).astype(BF16)
    dn = jnp.dot(act, wd_ref[...], preferred_element_type=F32)
    o_ref[...] = x1 + _rms(dn, g3_ref[...])


def _mix(x, yf, yl, yg, mg, wb, wo, wu, wd, g1, g2, g3, tm):
    n, d = x.shape
    row = lambda wd_: pl.BlockSpec((tm, wd_), lambda i: (i, 0))
    return pl.pallas_call(
        _mix_kernel,
        grid=(n // tm,),
        in_specs=[row(d), row(d), row(d), row(d), row(N_BRANCH * d),
                  _resident(wb.shape), _resident(wo.shape), _resident(wu.shape), _resident(wd.shape),
                  _resident(g1.shape), _resident(g2.shape), _resident(g3.shape)],
        out_specs=row(d),
        out_shape=jax.ShapeDtypeStruct((n, d), F32),
        compiler_params=_params(1),
        name="mix",
    )(x, yf, yl, yg, mg, wb, wo, wu, wd, g1, g2, g3)


def _layer(x, p, past, bsz, t):
    d = x.shape[1]
    prompt = past is None
    tm = min(256, bsz * t)
    if prompt:
        tq = tk = min(256, t)
        q, k, v, kb, vb, lx, lg, gq, gk, gv, gog, mg, misc = _proj(
            x, p['g_pre'], p['w_in'], tm, FOX_HEAD_DIM ** -0.5 * LOG2E, q_chunk=tq, v_chunk=tk)
    else:
        q, k, v, kb, vb, lx, lg, gq, gk, gv, gog, mg, misc = _proj(
            x, p['g_pre'], p['w_in'], tm, FOX_HEAD_DIM ** -0.5)

    misc3 = misc.reshape(bsz, t, MISC_WIDTH)
    fft = jnp.swapaxes(misc3[:, :, MISC_FF:MISC_FF + FOX_HEADS], 1, 2)
    if prompt:
        zc = jnp.zeros((bsz, 1, MISC_WIDTH), F32)
        zct = jnp.zeros((bsz, FOX_HEADS, 1), F32)
        logf, c, ct = _gate(misc3, fft, p['b_ff'], p['b_ff_t'], zc, zct, True, c_scale=LOG2E)
        y_fox = _fox_prompt(q, kb, vb, c.reshape(bsz * t, FOX_HEADS), ct, bsz, t, tq, tk)
        conv_buf = jnp.zeros((bsz, CONV_WIDTH - 1, LRU_WIDTH), F32)
        h0 = jnp.zeros((bsz, LRU_WIDTH), F32)
        s0t = jnp.zeros((bsz, GLA_HEADS, GLA_DV, GLA_DK), F32)
    else:
        k_c, v_c, layer, lf_c, conv_buf, h0, s0 = past
        pl_len = k_c.shape[2]
        k_c = k_c.reshape(k_c.shape[0], bsz, pl_len, FOX_WIDTH)
        v_c = v_c.reshape(v_c.shape[0], bsz, pl_len, FOX_WIDTH)
        lf_pad = jnp.pad(lf_c, ((0, 0), (0, 0), (0, MISC_WIDTH - FOX_HEADS)))
        zb = jnp.zeros((1, MISC_WIDTH), F32)
        zbt = jnp.zeros((FOX_HEADS, 1), F32)
        zc = jnp.zeros((bsz, 1, MISC_WIDTH), F32)
        zct = jnp.zeros((bsz, FOX_HEADS, 1), F32)
        _, c_past, cpt = _gate(lf_pad, jnp.swapaxes(lf_c, 1, 2), zb, zbt, zc, zct, False)
        tot = jnp.pad(c_past[:, pl_len - 1:pl_len, :], ((0, 0), (0, 0), (0, MISC_WIDTH - FOX_HEADS)))
        tot_t = cpt[:, :, pl_len - 1:pl_len]
        fft_pad = jnp.pad(fft, ((0, 0), (0, 0), (0, LANES - t)))
        logf, c, cnt = _gate(misc3, fft_pad, p['b_ff'], p['b_ff_t'], tot, tot_t, True)
        y_fox = _fox_decode(q.reshape(bsz, t, FOX_WIDTH), kb.reshape(bsz, t, FOX_WIDTH),
                            vb.reshape(bsz, t, FOX_WIDTH), k_c, v_c, layer, c, cnt, cpt)
        y_fox = y_fox.reshape(bsz * t, FOX_WIDTH)
        s0t = jnp.swapaxes(s0, 2, 3)

    lx3 = lx.reshape(bsz, t, LRU_WIDTH)
    y_lru, h_last = _lru(lx3, lg.reshape(bsz, t, LRU_WIDTH), conv_buf, h0, p['conv_w'], p['conv_b'],
                         p['w_a'], p['b_a'], p['w_x'], p['b_x'], p['lam'], min(128, t))
    assert t >= CONV_WIDTH - 1
    conv_new = lx3[:, t - (CONV_WIDTH - 1):].astype(F32)

    chunk = min(CHUNK, t)
    y_gla, s_t = _gla(gq, gk, gv, gog, misc, p['w_gate'], p['b_gate'], p['gla_norm'], s0t,
                      bsz, t, min(256, t), chunk)
    s_new = jnp.swapaxes(s_t, 2, 3)

    x = _mix(x, y_fox, y_lru.reshape(bsz * t, LRU_WIDTH), y_gla, mg, p['w_branch'], p['w_out'],
             p['w_up'], p['w_down'], p['g_post'], p['g_mlp_pre'], p['g_mlp_post'], tm)
    new_state = (k.reshape(bsz, t, FOX_HEADS, FOX_HEAD_DIM), v.reshape(bsz, t, FOX_HEADS, FOX_HEAD_DIM),
                 logf, conv_new, h_last, s_new)
    return x, new_state


def kernel(x_prompt, x_sample, cache_fox_k, cache_fox_v, cache_fox_logf, state_lru_conv, state_lru_h, state_gla, norm_mix_pre, norm_mix_post, w_in, b_fox_f, lru_conv_w, lru_conv_b, lru_w_a, lru_b_a, lru_w_x, lru_b_x, lru_lambda, gla_w_gate_up, gla_b_gate, gla_norm, w_branch, w_out, norm_mlp_pre, norm_mlp_post, w_up, w_down):
    depth, d = norm_mix_pre.shape
    bp, tp, _ = x_prompt.shape
    bs, ts, _ = x_sample.shape

    w_in_p = _pack_w_in(w_in)
    row = lambda a: a.reshape(depth, 1, a.shape[-1])
    b_ff = jnp.pad(b_fox_f, ((0, 0), (0, MISC_WIDTH - FOX_HEADS))).reshape(depth, 1, MISC_WIDTH)
    b_ff_t = b_fox_f.reshape(depth, FOX_HEADS, 1)
    w_gate = jnp.zeros((depth, MISC_WIDTH, GLA_QK_WIDTH), F32)
    w_gate = w_gate.at[:, MISC_GLR:MISC_GLR + GLA_RANK, :].set(gla_w_gate_up).astype(BF16)
    layers = []
    for l in range(depth):
        layers.append({
            'g_pre': row(norm_mix_pre)[l], 'w_in': w_in_p[l],
            'b_ff': b_ff[l], 'b_ff_t': b_ff_t[l],
            'conv_w': lru_conv_w[l], 'conv_b': row(lru_conv_b)[l],
            'w_a': lru_w_a[l].astype(BF16), 'b_a': row(lru_b_a)[l],
            'w_x': lru_w_x[l].astype(BF16), 'b_x': row(lru_b_x)[l], 'lam': row(lru_lambda)[l],
            'w_gate': w_gate[l], 'b_gate': row(gla_b_gate)[l], 'gla_norm': row(gla_norm)[l],
            'w_branch': w_branch[l].astype(BF16), 'w_out': w_out[l].astype(BF16),
            'w_up': w_up[l].astype(BF16), 'w_down': w_down[l].astype(BF16),
            'g_post': row(norm_mix_post)[l], 'g_mlp_pre': row(norm_mlp_pre)[l],
            'g_mlp_post': row(norm_mlp_post)[l],
        })

    hp = x_prompt.reshape(bp * tp, d)
    hs = x_sample.reshape(bs * ts, d)
    new_p = [[] for _ in range(6)]
    new_s = [[] for _ in range(6)]
    for l in range(depth):
        hp, st_p = _layer(hp, layers[l], None, bp, tp)
        past = (cache_fox_k, cache_fox_v, l, cache_fox_logf[l], state_lru_conv[l], state_lru_h[l],
                state_gla[l])
        hs, st_s = _layer(hs, layers[l], past, bs, ts)
        for lst, a in zip(new_p, st_p):
            lst.append(a)
        for lst, a in zip(new_s, st_s):
            lst.append(a)
    outs_p = [jnp.stack(a) for a in new_p]
    outs_s = [jnp.stack(a) for a in new_s]
    return (hp.reshape(bp, tp, d), hs.reshape(bs, ts, d), *outs_p, *outs_s)
```
